```python
import math
import jax
import jax.numpy as jnp
from jax import lax
import numpy as np

D_MODEL = 2048
BATCH = 4
SEQ = 4096
DEPTH = 4

GRID_W = 64
CTX_LEN = 256
D_MIX = D_MODEL
HY_W = D_MIX // 4
SSD_W = D_MIX // 4
DA_W = D_MIX // 2
HY_ORDER = 2
HY_CONV = 3
HY_EMB = 33
HY_BANDS = (HY_EMB - 1) // 2
HY_HID = 64
HY_MIN_DECAY = math.log(1e-2) / 1.5
HY_MAX_DECAY = math.log(1e-2) / 0.3
HY_PROJ = (HY_ORDER + 1) * HY_W
SSD_HD = 64
SSD_H = SSD_W // SSD_HD
SSD_G = 2
SSD_N = 128
SSD_CONV = 3
SSD_CHUNK = 128
SSD_XBC = SSD_W + 2 * SSD_G * SSD_N
SSD_PROJ = SSD_W + SSD_XBC + 2 * SSD_H
DA_H = 8
DA_DV = DA_W // DA_H
DA_DK = DA_DV // 2
DA_QK = DA_H * 2 * DA_DK
DA_PROJ = 2 * DA_QK + DA_W
Q_BLOCK = 128
ROPE_THETA = 10000.0
ROPE_PAIRS = DA_DK // 4
N_IN = HY_PROJ + SSD_PROJ + DA_PROJ
N_EXPERTS = 16
EC_FACTOR = 2
D_FF = 1536
ALPHA = (2 * DEPTH) ** 0.25
BETA = (8 * DEPTH) ** -0.25
LN_EPS = 1e-6

kernel_name = 'hybrid_hyena_ssd_diffattn_ec_moe'


def layer_norm(x, g=None, b=None):
    xf = x.astype(jnp.float32)
    mu = jnp.mean(xf, axis=-1, keepdims=True)
    var = jnp.mean(jnp.square(xf - mu), axis=-1, keepdims=True)
    y = (xf - mu) * lax.rsqrt(var + LN_EPS)
    if g is not None:
        y = y * g.astype(jnp.float32) + b.astype(jnp.float32)
    return y.astype(x.dtype)


def rms_norm(x, g):
    xf = x.astype(jnp.float32)
    y = xf * lax.rsqrt(jnp.mean(jnp.square(xf), axis=-1, keepdims=True) + LN_EPS)
    return (y * g.astype(jnp.float32)).astype(x.dtype)


def modulate(h, shift, scale):
    return h * (1 + scale) + shift


def short_conv(u, w, b):
    y = lax.conv_general_dilated(u, w[:, None, :].astype(u.dtype), window_strides=(1,), padding='SAME',
                                 dimension_numbers=('NWC', 'WIO', 'NWC'), feature_group_count=u.shape[-1])
    return y + b.astype(u.dtype)


def axial_rope(rows):
    r = jnp.repeat(jnp.arange(rows, dtype=jnp.float32), GRID_W)
    col = jnp.tile(jnp.arange(GRID_W, dtype=jnp.float32), rows)
    inv = ROPE_THETA ** (-jnp.arange(ROPE_PAIRS, dtype=jnp.float32) / ROPE_PAIRS)
    ang = jnp.stack([r[:, None] * inv, col[:, None] * inv], axis=1)
    return jnp.cos(ang), jnp.sin(ang)


def apply_rope(x, cos, sin):
    xr = x.reshape(*x.shape[:-1], 2, 2, ROPE_PAIRS).astype(jnp.float32)
    a, b = xr[..., 0, :], xr[..., 1, :]
    c, s = cos[None, :, None, None], sin[None, :, None, None]
    out = jnp.stack([a * c - b * s, b * c + a * s], axis=-2)
    return out.reshape(x.shape).astype(x.dtype)


def hyena_filters(L, w1, b1, w2, b2, w3, freq):
    t = jnp.arange(L, dtype=jnp.float32)
    t_unit = t / max(L - 1, 1)
    bands = jnp.linspace(1e-4, HY_BANDS - 1, HY_BANDS, dtype=jnp.float32)
    ang = (2.0 * math.pi / L) * t[:, None] * bands[None, :]
    feats = jnp.concatenate([t_unit[:, None], jnp.cos(ang), -jnp.sin(ang)], axis=-1)
    f = freq.astype(jnp.float32)
    h = jnp.sin(f * (feats @ w1.astype(jnp.float32) + b1.astype(jnp.float32)))
    h = jnp.sin(f * (h @ w2.astype(jnp.float32) + b2.astype(jnp.float32)))
    h = (h @ w3.astype(jnp.float32)).reshape(L, 2, HY_ORDER, HY_W)
    deltas = jnp.abs(jnp.linspace(HY_MIN_DECAY, HY_MAX_DECAY, HY_W, dtype=jnp.float32))
    h = h * jnp.exp(-t_unit[:, None, None, None] * deltas)
    fwd, bwd = h[:, 0], h[:, 1]
    k = jnp.concatenate([fwd, jnp.zeros_like(fwd[:1]), bwd[:0:-1]], axis=0)
    return k / jnp.sum(jnp.abs(k), axis=0, keepdims=True)


def long_conv(z, k, skip):
    L = z.shape[1]
    zf = jnp.fft.rfft(z.astype(jnp.float32), n=2 * L, axis=1)
    kf = jnp.fft.rfft(k, axis=0)
    y = jnp.fft.irfft(zf * kf[None], n=2 * L, axis=1)[:, :L]
    return (y + z.astype(jnp.float32) * skip.astype(jnp.float32)).astype(z.dtype)


def hyena_branch(p, conv_w, conv_b, w1, b1, w2, b2, w3, freq, skip):
    L = p.shape[1]
    v, x1, x2 = jnp.split(short_conv(p, conv_w, conv_b), HY_ORDER + 1, axis=-1)
    k = hyena_filters(L, w1, b1, w2, b2, w3, freq)
    z = x1 * long_conv(v, k[:, 0], skip[0])
    return x2 * long_conv(z, k[:, 1], skip[1])


def ssd_scan(xh, dt, A, bm, cm, h0):
    b, L, H, P = xh.shape
    nc, Q = L // SSD_CHUNK, SSD_CHUNK
    rep = H // SSD_G
    bh = jnp.repeat(bm.astype(jnp.float32), rep, axis=2).reshape(b, nc, Q, H, SSD_N)
    ch = jnp.repeat(cm.astype(jnp.float32), rep, axis=2).reshape(b, nc, Q, H, SSD_N)
    xdt = (xh.astype(jnp.float32) * dt[..., None]).reshape(b, nc, Q, H, P)
    a_cum = jnp.cumsum((dt * A).reshape(b, nc, Q, H), axis=2)
    mask = jnp.tril(jnp.ones((Q, Q), dtype=bool))[None, None, :, :, None]
    seg = a_cum[:, :, :, None, :] - a_cum[:, :, None, :, :]
    lmat = jnp.exp(jnp.where(mask, seg, -jnp.inf))
    scores = jnp.einsum('bcihn,bcjhn->bcijh', ch, bh) * lmat
    y_diag = jnp.einsum('bcijh,bcjhp->bcihp', scores, xdt)
    decay_to_end = jnp.exp(a_cum[:, :, -1:, :] - a_cum)
    states = jnp.einsum('bcjhn,bcjh,bcjhp->bchpn', bh, decay_to_end, xdt)
    chunk_decay = jnp.exp(a_cum[:, :, -1, :])

    def step(h, inp):
        st, dec = inp
        return dec[..., None, None] * h + st, h

    h_final, h_start = lax.scan(step, h0, (states.swapaxes(0, 1), chunk_decay.swapaxes(0, 1)))
    h_start = h_start.swapaxes(0, 1)
    y_off = jnp.einsum('bcihn,bchpn,bcih->bcihp', ch, h_start, jnp.exp(a_cum))
    return (y_diag + y_off).reshape(b, L, H, P), h_final


def ssd_prep(p, conv_w, conv_b, dt_bias):
    b, L, _ = p.shape
    z, xbc, dt = jnp.split(p, [SSD_W, SSD_W + SSD_XBC], axis=-1)
    xbc = jax.nn.silu(short_conv(xbc, conv_w, conv_b))
    xs, bm, cm = jnp.split(xbc, [SSD_W, SSD_W + SSD_G * SSD_N], axis=-1)
    xs = xs.reshape(b, L, SSD_H, SSD_HD)
    bm = bm.reshape(b, L, SSD_G, SSD_N)
    cm = cm.reshape(b, L, SSD_G, SSD_N)
    dt = jax.nn.softplus(dt.reshape(b, L, 2, SSD_H).astype(jnp.float32) + dt_bias.astype(jnp.float32))
    return z, xs, bm, cm, dt


def ssd_bidir(xs, bm, cm, dt, A, d_skip, h0f, h0b):
    flip = lambda a: jnp.flip(a, axis=1)
    yf, hf = ssd_scan(xs, dt[:, :, 0], A[0], bm, cm, h0f)
    yb, hb = ssd_scan(flip(xs), flip(dt[:, :, 1]), A[1], flip(bm), flip(cm), h0b)
    y = yf + flip(yb) + d_skip.astype(jnp.float32)[:, None] * xs.astype(jnp.float32)
    return y, hf, hb


def ssd_branch(p_ctx, p_lat, with_ctx, conv_w, conv_b, a_log, dt_bias, d_skip, norm_g):
    A = -jnp.exp(a_log.astype(jnp.float32))

    def gated_out(z, y):
        b, L = z.shape[:2]
        y = y.reshape(b, L, SSD_W).astype(z.dtype) * jax.nn.silu(z)
        return rms_norm(y, norm_g)

    zc, xc, bc, cc, dtc = ssd_prep(p_ctx, conv_w, conv_b, dt_bias)
    h0 = jnp.zeros((p_ctx.shape[0], SSD_H, SSD_HD, SSD_N), jnp.float32)
    yc, hf, hb = ssd_bidir(xc, bc, cc, dtc, A, d_skip, h0, h0)
    zl, xl, bl, cl, dtl = ssd_prep(p_lat, conv_w, conv_b, dt_bias)
    yl, _, _ = ssd_bidir(xl, bl, cl, dtl, A, d_skip, hf, hb)
    y_ctx = gated_out(zc, yc) if with_ctx else None
    return y_ctx, gated_out(zl, yl)


def diff_attend(q, k, v, lam):
    s = jnp.einsum('bqhmd,bkhmd->bhmqk', q, k).astype(jnp.float32) * (DA_DK ** -0.5)
    a = jax.nn.softmax(s, axis=-1)
    a = a[:, :, 0] - lam * a[:, :, 1]
    return jnp.einsum('bhqk,bkhe->bqhe', a.astype(v.dtype), v)


def diff_attn_branch(p_ctx, p_lat, with_ctx, cos, sin, lam_p, subln_g, lam_init):
    def split_heads(p):
        b, L, _ = p.shape
        q, k, v = jnp.split(p, [DA_QK, 2 * DA_QK], axis=-1)
        return (q.reshape(b, L, DA_H, 2, DA_DK), k.reshape(b, L, DA_H, 2, DA_DK),
                v.reshape(b, L, DA_H, DA_DV))

    qc, kc, vc = split_heads(p_ctx)
    ql, kl, vl = split_heads(p_lat)
    ql = apply_rope(ql, cos, sin)
    kl = apply_rope(kl, cos, sin)
    lp = lam_p.astype(jnp.float32)
    lam = jnp.exp(jnp.sum(lp[0] * lp[1])) - jnp.exp(jnp.sum(lp[2] * lp[3])) + lam_init
    k_all = jnp.concatenate([kl, kc], axis=1)
    v_all = jnp.concatenate([vl, vc], axis=1)
    b, L = ql.shape[:2]
    qb = ql.reshape(b, L // Q_BLOCK, Q_BLOCK, DA_H, 2, DA_DK).swapaxes(0, 1)
    ob = lax.map(lambda qi: diff_attend(qi, k_all, v_all, lam), qb)
    o_lat = ob.swapaxes(0, 1).reshape(b, L, DA_H, DA_DV)

    def finish(o):
        o = rms_norm(o, subln_g) * (1.0 - lam_init)
        return o.reshape(*o.shape[:2], DA_W)

    y_ctx = finish(diff_attend(qc, kc, vc, lam)) if with_ctx else None
    return y_ctx, finish(o_lat)


def hybrid_mixer(h_ctx, h_lat, with_ctx, cos, sin, lam_init, w_in, w_out,
                 hy_conv_w, hy_conv_b, hy_w1, hy_b1, hy_w2, hy_b2, hy_w3, hy_freq, hy_skip,
                 ssd_conv_w, ssd_conv_b, ssd_a_log, ssd_dt_bias, ssd_d, ssd_norm_g,
                 da_lambda, da_subln_g):
    p_ctx = jnp.einsum('bld,de->ble', h_ctx, w_in)
    p_lat = jnp.einsum('bld,de->ble', h_lat, w_in)
    hy_c, ssd_c, da_c = jnp.split(p_ctx, [HY_PROJ, HY_PROJ + SSD_PROJ], axis=-1)
    hy_l, ssd_l, da_l = jnp.split(p_lat, [HY_PROJ, HY_PROJ + SSD_PROJ], axis=-1)
    hy_args = (hy_conv_w, hy_conv_b, hy_w1, hy_b1, hy_w2, hy_b2, hy_w3, hy_freq, hy_skip)
    y_hy_l = hyena_branch(hy_l, *hy_args)
    y_ssd_c, y_ssd_l = ssd_branch(ssd_c, ssd_l, with_ctx, ssd_conv_w, ssd_conv_b, ssd_a_log,
                                  ssd_dt_bias, ssd_d, ssd_norm_g)
    y_da_c, y_da_l = diff_attn_branch(da_c, da_l, with_ctx, cos, sin, da_lambda, da_subln_g, lam_init)
    y_lat = jnp.concatenate([y_hy_l, y_ssd_l, y_da_l], axis=-1) @ w_out
    y_ctx = None
    if with_ctx:
        y_hy_c = hyena_branch(hy_c, *hy_args)
        y_ctx = jnp.concatenate([y_hy_c, y_ssd_c, y_da_c], axis=-1) @ w_out
    return y_ctx, y_lat


def ec_moe(h, router, w_gate, w_up, w_down):
    b, n, d = h.shape
    cap = EC_FACTOR * n // N_EXPERTS
    aff = jax.nn.softmax(jnp.einsum('bnd,de->bne', h, router).astype(jnp.float32), axis=-1)
    top_w, top_idx = lax.top_k(aff.swapaxes(1, 2), cap)
    xs = jax.vmap(lambda hb, ib: hb[ib])(h, top_idx)
    g = jnp.einsum('becd,edf->becf', xs, w_gate)
    u = jnp.einsum('becd,edf->becf', xs, w_up)
    out = jnp.einsum('becf,efd->becd', jax.nn.silu(g) * u, w_down) * top_w[..., None].astype(h.dtype)
    return jax.vmap(lambda ib, ob: jnp.zeros((n, d), ob.dtype).at[ib.reshape(-1)].add(ob.reshape(-1, d)))(top_idx, out)


def setup_inputs(seed: int = 0) -> dict:
    key = jax.random.key(seed)
    ks = jax.random.split(key, 40)
    nrm = lambda i, shape, s: jax.random.normal(ks[i], shape, jnp.float32) * s
    L = DEPTH
    dt0 = jnp.exp(jax.random.uniform(ks[14], (L, 2, SSD_H), jnp.float32, math.log(1e-3), math.log(1e-1)))
    return {
        'x': nrm(0, (BATCH, SEQ, D_MODEL), 1.0),
        'c': nrm(1, (BATCH, D_MODEL), 1.0),
        'ctx': nrm(2, (BATCH, CTX_LEN, D_MODEL), 1.0),
        'c_ctx': nrm(3, (D_MODEL,), 1.0),
        'w_mod': nrm(4, (L, D_MODEL, 6 * D_MODEL), 0.5 * D_MODEL ** -0.5),
        'b_mod': nrm(5, (L, 6 * D_MODEL), 0.02),
        'w_in': nrm(6, (L, D_MODEL, N_IN), D_MODEL ** -0.5),
        'hy_conv_w': nrm(7, (L, HY_CONV, HY_PROJ), HY_CONV ** -0.5),
        'hy_conv_b': nrm(8, (L, HY_PROJ), 0.02),
        'hy_w1': nrm(9, (L, HY_EMB, HY_HID), HY_EMB ** -0.5),
        'hy_b1': nrm(10, (L, HY_HID), 0.02),
        'hy_w2': nrm(11, (L, HY_HID, HY_HID), HY_HID ** -0.5),
        'hy_b2': nrm(12, (L, HY_HID), 0.02),
        'hy_w3': nrm(13, (L, HY_HID, 2 * HY_ORDER * HY_W), HY_HID ** -0.5),
        'hy_freq': 1.0 + nrm(15, (L, HY_HID), 0.1),
        'hy_skip': nrm(16, (L, HY_ORDER, HY_W), 0.5),
        'ssd_conv_w': nrm(17, (L, SSD_CONV, SSD_XBC), SSD_CONV ** -0.5),
        'ssd_conv_b': nrm(18, (L, SSD_XBC), 0.02),
        'ssd_a_log': jnp.log(jax.random.uniform(ks[19], (L, 2, SSD_H), jnp.float32, 1.0, 16.0)),
        'ssd_dt_bias': dt0 + jnp.log(-jnp.expm1(-dt0)),
        'ssd_d': 1.0 + nrm(20, (L, SSD_H), 0.1),
        'ssd_norm_g': 1.0 + nrm(21, (L, SSD_W), 0.02),
        'da_lambda': nrm(22, (L, 4, DA_DK), 0.1),
        'da_subln_g': 1.0 + nrm(23, (L, DA_DV), 0.02),
        'w_out': nrm(24, (L, D_MIX, D_MODEL), BETA * D_MIX ** -0.5),
        'ln1_g': 1.0 + nrm(25, (L, D_MODEL), 0.02),
        'ln1_b': nrm(26, (L, D_MODEL), 0.02),
        'router': nrm(27, (L, D_MODEL, N_EXPERTS), D_MODEL ** -0.5),
        'w_gate': nrm(28, (L, N_EXPERTS, D_MODEL, D_FF), D_MODEL ** -0.5),
        'w_up': nrm(29, (L, N_EXPERTS, D_MODEL, D_FF), D_MODEL ** -0.5),
        'w_down': nrm(30, (L, N_EXPERTS, D_FF, D_MODEL), BETA * D_FF ** -0.5),
        'ln2_g': 1.0 + nrm(31, (L, D_MODEL), 0.02),
        'ln2_b': nrm(32, (L, D_MODEL), 0.02),
    }


def reference(x, c, ctx, c_ctx, w_mod, b_mod, w_in, hy_conv_w, hy_conv_b, hy_w1, hy_b1, hy_w2, hy_b2,
              hy_w3, hy_freq, hy_skip, ssd_conv_w, ssd_conv_b, ssd_a_log, ssd_dt_bias, ssd_d, ssd_norm_g,
              da_lambda, da_subln_g, w_out, ln1_g, ln1_b, router, w_gate, w_up, w_down, ln2_g, ln2_b):
    rows = x.shape[1] // GRID_W
    cos, sin = axial_rope(rows)
    s_lat = jax.nn.silu(c)
    s_ctx = jax.nn.silu(c_ctx)
    xc = ctx
    for l in range(DEPTH):
        with_ctx = l < DEPTH - 1
        lam_init = 0.8 - 0.6 * math.exp(-0.3 * l)
        mod = (s_lat @ w_mod[l] + b_mod[l])[:, None, :]
        mod_c = s_ctx @ w_mod[l] + b_mod[l]
        sh1, sc1, g1, sh2, sc2, g2 = jnp.split(mod, 6, axis=-1)
        sh1c, sc1c, g1c, sh2c, sc2c, g2c = jnp.split(mod_c, 6, axis=-1)
        h_lat = modulate(layer_norm(x), sh1, sc1)
        h_ctx = modulate(layer_norm(xc), sh1c, sc1c)
        y_ctx, y_lat = hybrid_mixer(h_ctx, h_lat, with_ctx, cos, sin, lam_init, w_in[l], w_out[l],
                                    hy_conv_w[l], hy_conv_b[l], hy_w1[l], hy_b1[l], hy_w2[l], hy_b2[l],
                                    hy_w3[l], hy_freq[l], hy_skip[l],
                                    ssd_conv_w[l], ssd_conv_b[l], ssd_a_log[l], ssd_dt_bias[l], ssd_d[l],
                                    ssd_norm_g[l], da_lambda[l], da_subln_g[l])
        x = layer_norm(ALPHA * x + g1 * y_lat, ln1_g[l], ln1_b[l])
        h = modulate(layer_norm(x), sh2, sc2)
        x = layer_norm(ALPHA * x + g2 * ec_moe(h, router[l], w_gate[l], w_up[l], w_down[l]), ln2_g[l], ln2_b[l])
        if with_ctx:
            xc = layer_norm(ALPHA * xc + g1c * y_ctx, ln1_g[l], ln1_b[l])
            hc = modulate(layer_norm(xc), sh2c, sc2c)
            xc = layer_norm(ALPHA * xc + g2c * ec_moe(hc, router[l], w_gate[l], w_up[l], w_down[l]),
                            ln2_g[l], ln2_b[l])
    return x
```

```python
import functools
import math

import numpy as np
import jax
import jax.numpy as jnp
from jax import lax
from jax.experimental import pallas as pl
from jax.experimental.pallas import tpu as pltpu

F32 = jnp.float32
BF16 = jnp.bfloat16
HIGHEST = lax.Precision.HIGHEST

D = 2048
B = 4
SEQ = 4096
DEPTH = 4
GRID_W = 64
CTX = 256
HY_W = 512
SSD_W = 512
DA_W = 1024
HY_EMB = 33
HY_BANDS = 16
HY_HID = 64
HY_MIN_DECAY = math.log(1e-2) / 1.5
HY_MAX_DECAY = math.log(1e-2) / 0.3
HY_PROJ = 3 * HY_W
SSD_H = 8
SSD_N = 128
SSD_CHUNK = 128
SSD_XBC = 1024
DA_H = 8
DA_DK = 64
DA_QK = 1024
ROPE_THETA = 10000.0
ROPE_PAIRS = 16
N_EXPERTS = 16
EC_FACTOR = 2
D_FF = 1536
ALPHA = (2 * DEPTH) ** 0.25
LN_EPS = 1e-6

NL = B * SEQ
NC = B * CTX
M = NL + NC
LANES = 128
TM = 256
VMEM_LIMIT = 56 * 1024 * 1024

N_LAT_TILES = NL // TM
TILES_PER_SAMPLE = SEQ // TM


def _mod_row(i):
    return jnp.where(i < N_LAT_TILES, i // TILES_PER_SAMPLE, B)


def _mod_spec(chunk):
    return pl.BlockSpec((None, 1, D), lambda i, chunk=chunk: (_mod_row(i), 0, chunk))


def _params(*sem):
    return pltpu.CompilerParams(dimension_semantics=sem, vmem_limit_bytes=VMEM_LIMIT)


def _ln(x):
    mu = jnp.mean(x, axis=-1, keepdims=True)
    xc = x - mu
    var = jnp.mean(xc * xc, axis=-1, keepdims=True)
    return xc * lax.rsqrt(var + LN_EPS)


def _silu(x):
    return x * jax.nn.sigmoid(x)


def _mod_kernel(c_ref, w_ref, b_ref, o_ref):
    s = _silu(c_ref[...])
    o_ref[...] = jnp.dot(s, w_ref[...], precision=HIGHEST, preferred_element_type=F32) + b_ref[...]


def modulation(cc, w_mod, b_mod):
    tn = 1024
    return pl.pallas_call(
        _mod_kernel,
        grid=(DEPTH, 6 * D // tn),
        in_specs=[
            pl.BlockSpec((8, D), lambda l, j: (0, 0)),
            pl.BlockSpec((None, D, tn), lambda l, j: (l, 0, j)),
            pl.BlockSpec((None, 1, tn), lambda l, j: (l, 0, j)),
        ],
        out_specs=pl.BlockSpec((None, 8, tn), lambda l, j: (l, 0, j)),
        out_shape=jax.ShapeDtypeStruct((DEPTH, 8, 6 * D), F32),
        compiler_params=_params("arbitrary", "arbitrary"),
        name="modulation",
    )(cc, w_mod, b_mod.reshape(DEPTH, 1, 6 * D))


def _inproj_kernel(x_ref, sh_ref, sc_ref, w_ref, *rest, mode):
    h = _ln(x_ref[...]) * (1.0 + sc_ref[...]) + sh_ref[...]
    acc = jnp.dot(h.astype(BF16), w_ref[...], preferred_element_type=F32)
    if mode == "hy":
        (o_ref,) = rest
        o_ref[...] = acc.astype(o_ref.dtype)
    elif mode == "ssd":
        o_ref, dt_ref = rest
        o_ref[...] = acc[:, : SSD_XBC + SSD_W].astype(o_ref.dtype)
        dt_ref[...] = acc[:, SSD_XBC + SSD_W:]
    else:
        cos_ref, sin_ref, o_ref = rest
        cs = cos_ref[...]
        sn = sin_ref[...]
        lane = lax.broadcasted_iota(jnp.int32, (TM, LANES), 1)
        is_a = (lane % 32) < ROPE_PAIRS
        for g in range(2 * DA_QK // LANES):
            blk = acc[:, g * LANES:(g + 1) * LANES]
            swapped = jnp.where(is_a, pltpu.roll(blk, LANES - ROPE_PAIRS, 1), pltpu.roll(blk, ROPE_PAIRS, 1))
            rot = blk * cs + swapped * sn
            if g < DA_QK // LANES:
                rot = rot * (DA_DK ** -0.5)
            o_ref[:, g * LANES:(g + 1) * LANES] = rot.astype(o_ref.dtype)
        o_ref[:, 2 * DA_QK:] = acc[:, 2 * DA_QK:].astype(o_ref.dtype)


def in_projection(x, mod_l, w, mode, rope=None):
    n = w.shape[1]
    in_specs = [
        pl.BlockSpec((TM, D), lambda i: (i, 0)),
        _mod_spec(0),
        _mod_spec(1),
        pl.BlockSpec((D, n), lambda i: (0, 0)),
    ]
    args = [x, mod_l, mod_l, w]
    if mode == "hy":
        out_shape = jax.ShapeDtypeStruct((M, n), BF16)
        out_specs = pl.BlockSpec((TM, n), lambda i: (i, 0))
    elif mode == "ssd":
        n0 = SSD_XBC + SSD_W
        out_shape = (jax.ShapeDtypeStruct((M, n0), BF16), jax.ShapeDtypeStruct((M, LANES), F32))
        out_specs = (pl.BlockSpec((TM, n0), lambda i: (i, 0)), pl.BlockSpec((TM, LANES), lambda i: (i, 0)))
    else:
        rope_idx = lambda i: (jnp.where(i < N_LAT_TILES, i % TILES_PER_SAMPLE, TILES_PER_SAMPLE), 0)
        in_specs += [pl.BlockSpec((TM, LANES), rope_idx), pl.BlockSpec((TM, LANES), rope_idx)]
        args += list(rope)
        out_shape = jax.ShapeDtypeStruct((M, n), BF16)
        out_specs = pl.BlockSpec((TM, n), lambda i: (i, 0))
    return pl.pallas_call(
        functools.partial(_inproj_kernel, mode=mode),
        grid=(M // TM,),
        in_specs=in_specs,
        out_specs=out_specs,
        out_shape=out_shape,
        compiler_params=_params("arbitrary"),
        name="inproj_" + mode,
    )(*args)


def rope_tables():
    pos = np.arange(SEQ)
    r = (pos // GRID_W).astype(np.float64)
    col = (pos % GRID_W).astype(np.float64)
    inv = ROPE_THETA ** (-np.arange(ROPE_PAIRS, dtype=np.float64) / ROPE_PAIRS)
    inv = inv.astype(np.float32).astype(np.float64)
    lane = np.arange(LANES)
    d = lane % DA_DK
    axis = d // 32
    pair = d % ROPE_PAIRS
    is_b = (d % 32) >= ROPE_PAIRS
    ang = np.where(axis[None, :] == 0, r[:, None], col[:, None]) * inv[pair][None, :]
    cos = np.cos(ang)
    sin = np.where(is_b[None, :], np.sin(ang), -np.sin(ang))
    cos = np.concatenate([cos, np.ones((TM, LANES))], axis=0)
    sin = np.concatenate([sin, np.zeros((TM, LANES))], axis=0)
    return jnp.asarray(cos, F32), jnp.asarray(sin, F32)


def _attn_kernel(lam_ref, q_ref, *rest, n_seg, post_scale):
    kv = rest[: 2 * n_seg]
    g_ref, o_ref = rest[2 * n_seg:]
    lam = lam_ref[0]
    q = q_ref[...]
    lane = lax.broadcasted_iota(jnp.int32, q.shape, 1)
    zero = jnp.zeros_like(q)
    qm = (jnp.where(lane < DA_DK, q, zero), jnp.where(lane >= DA_DK, q, zero))
    nt = (((1,), (1,)), ((), ()))
    scores = [[lax.dot_general(qm[m], kv[2 * s][...], nt, preferred_element_type=F32) for s in range(n_seg)]
              for m in range(2)]
    probs = []
    for m in range(2):
        mx = functools.reduce(jnp.maximum, [jnp.max(s, axis=-1, keepdims=True) for s in scores[m]])
        es = [jnp.exp(s - mx) for s in scores[m]]
        tot = functools.reduce(lambda a, b: a + b, [jnp.sum(e, axis=-1, keepdims=True) for e in es])
        probs.append((es, 1.0 / tot))
    out = None
    for s in range(n_seg):
        a = probs[0][0][s] * probs[0][1] - (lam * probs[1][1]) * probs[1][0][s]
        o = jnp.dot(a.astype(BF16), kv[2 * s + 1][...], preferred_element_type=F32)
        out = o if out is None else out + o
    out = out * lax.rsqrt(jnp.mean(out * out, axis=-1, keepdims=True) + LN_EPS)
    o_ref[...] = ((out * g_ref[...]) * post_scale).astype(o_ref.dtype)


def diff_attention(da, lam, subln_g, lam_init, ctx_queries):
    tq = 256
    lam = jnp.reshape(lam, (1,)).astype(F32)
    g = subln_g.reshape(1, LANES)
    kc_spec = pl.BlockSpec((CTX, LANES), lambda b, h, i: (NL // CTX + b, DA_H + h))
    vc_spec = pl.BlockSpec((CTX, LANES), lambda b, h, i: (NL // CTX + b, 2 * DA_H + h))
    if ctx_queries:
        rows, nq = NC, CTX // tq
        q_spec = pl.BlockSpec((tq, LANES), lambda b, h, i: ((NL + b * CTX) // tq + i, h))
        segs, seg_args = [kc_spec, vc_spec], [da, da]
    else:
        rows, nq = NL, SEQ // tq
        q_spec = pl.BlockSpec((tq, LANES), lambda b, h, i: (b * nq + i, h))
        kl_spec = pl.BlockSpec((SEQ, LANES), lambda b, h, i: (b, DA_H + h))
        vl_spec = pl.BlockSpec((SEQ, LANES), lambda b, h, i: (b, 2 * DA_H + h))
        segs, seg_args = [kl_spec, vl_spec, kc_spec, vc_spec], [da, da, da, da]
    return pl.pallas_call(
        functools.partial(_attn_kernel, n_seg=len(segs) // 2, post_scale=1.0 - lam_init),
        grid=(B, DA_H, nq),
        in_specs=[pl.BlockSpec(memory_space=pltpu.SMEM), q_spec] + segs + [pl.BlockSpec((1, LANES), lambda b, h, i: (0, 0))],
        out_specs=pl.BlockSpec((tq, LANES), lambda b, h, i: (b * nq + i, h)),
        out_shape=jax.ShapeDtypeStruct((rows, DA_W), BF16),
        compiler_params=_params("arbitrary", "arbitrary", "arbitrary"),
        name="diff_attn_ctx" if ctx_queries else "diff_attn_lat",
    )(lam, da, *seg_args, g)


SSD_NC_CTX = CTX // SSD_CHUNK
SSD_NC_LAT = SEQ // SSD_CHUNK
SSD_STEPS = SSD_NC_CTX + SSD_NC_LAT


def _ssd_chunk(s, direction):
    in_ctx = s < SSD_NC_CTX
    k = jnp.where(in_ctx, s, s - SSD_NC_CTX)
    nck = jnp.where(in_ctx, SSD_NC_CTX, SSD_NC_LAT)
    if direction == 1:
        k = nck - 1 - k
    return in_ctx, k, nck


def _ssd_rowblock(b, s, direction):
    in_ctx, k, _ = _ssd_chunk(s, direction)
    return jnp.where(in_ctx, NL // SSD_CHUNK + b * SSD_NC_CTX + k, b * SSD_NC_LAT + k)


def _ssd_kernel(xf_ref, xfp_ref, xfn_ref, dtf_ref, xb_ref, xbp_ref, xbn_ref, dtb_ref,
                cw_ref, cb_ref, a_ref, dtbias_ref, dskip_ref, yf_ref, yb_ref, state_ref):
    s = pl.program_id(1)
    q = SSD_CHUNK

    @pl.when(s == 0)
    def _():
        state_ref[...] = jnp.zeros_like(state_ref)

    row = lax.broadcasted_iota(jnp.int32, (q, 1), 0)
    ri = lax.broadcasted_iota(jnp.int32, (q, q), 0)
    ci = lax.broadcasted_iota(jnp.int32, (q, q), 1)
    lane_lo = lax.broadcasted_iota(jnp.int32, (1, LANES), 1) < 64
    cw = cw_ref[...]
    dirs = ((xf_ref, xfp_ref, xfn_ref, dtf_ref, yf_ref), (xb_ref, xbp_ref, xbn_ref, dtb_ref, yb_ref))
    for d, (x_ref, xp_ref, xn_ref, dt_ref, y_ref) in enumerate(dirs):
        _, k, nck = _ssd_chunk(s, d)
        u = x_ref[...].astype(F32)
        prev = jnp.where(k == 0, 0.0, xp_ref[15:16, :].astype(F32))
        nxt = jnp.where(k == nck - 1, 0.0, xn_ref[0:1, :].astype(F32))
        up = jnp.where(row == 0, prev, pltpu.roll(u, 1, 0))
        dn = jnp.where(row == q - 1, nxt, pltpu.roll(u, q - 1, 0))
        conv = cw[0:1, :] * up + cw[1:2, :] * u + cw[2:3, :] * dn + cb_ref[...]
        xbc = _silu(conv)
        xs = xbc[:, :SSD_W]
        bm = xbc[:, SSD_W:SSD_W + 2 * SSD_N]
        cm = xbc[:, SSD_W + 2 * SSD_N:]
        z = dt_ref[...] + dtbias_ref[...]
        dt = jnp.maximum(z, 0.0) + jnp.log(1.0 + jnp.exp(-jnp.abs(z)))
        a = dt * a_ref[...]
        tri = (ci <= ri) if d == 0 else (ci >= ri)
        a_cum = jnp.dot(tri.astype(F32), a, precision=HIGHEST, preferred_element_type=F32)
        a_cum_t = a_cum.T
        ea = jnp.exp(a_cum)
        a_tot = a_cum[q - 1:q, :] if d == 0 else a_cum[0:1, :]
        to_end = jnp.exp(a_tot - a_cum)
        chunk_decay = jnp.exp(a_tot)

        def halves(col_arr, c0, c1):
            return jnp.where(lane_lo, col_arr[:, c0:c0 + 1], col_arr[:, c1:c1 + 1])

        for g in range(2):
            bg = bm[:, g * SSD_N:(g + 1) * SSD_N]
            cgb = cm[:, g * SSD_N:(g + 1) * SSD_N].astype(BF16)
            gmat = lax.dot_general(cgb, bg.astype(BF16), (((1,), (1,)), ((), ())), preferred_element_type=F32)
            bgt = bg.T.astype(BF16)
            for pr in range(2):
                p = 2 * g + pr
                c0, c1 = 8 * d + 2 * p, 8 * d + 2 * p + 1
                xpair = xs[:, p * LANES:(p + 1) * LANES]
                xdt = xpair * halves(dt, c0, c1)
                y = None
                for hh, hc in ((0, c0), (1, c1)):
                    seg = a_cum[:, hc:hc + 1] - a_cum_t[hc:hc + 1, :]
                    lmat = jnp.exp(jnp.where(tri, seg, -jnp.inf))
                    xm = jnp.where(lane_lo if hh == 0 else jnp.logical_not(lane_lo), xdt, 0.0)
                    t = jnp.dot((gmat * lmat).astype(BF16), xm.astype(BF16), preferred_element_type=F32)
                    y = t if y is None else y + t
                st = state_ref[d, p]
                y = y + halves(ea, c0, c1) * jnp.dot(cgb, st.astype(BF16), preferred_element_type=F32)
                upd = jnp.dot(bgt, (xdt * halves(to_end, c0, c1)).astype(BF16), preferred_element_type=F32)
                state_ref[d, p] = halves(chunk_decay, c0, c1) * st + upd
                if d == 0:
                    y = y + dskip_ref[:, p * LANES:(p + 1) * LANES] * xpair
                y_ref[:, p * LANES:(p + 1) * LANES] = y


def ssd_scan(zx, dt, conv_w, conv_b, a_log, dt_bias, d_skip):
    n16 = M // 16

    def chunk_spec(direction):
        return pl.BlockSpec((SSD_CHUNK, SSD_XBC), lambda b, s: (_ssd_rowblock(b, s, direction), 0))

    def prev_spec(direction):
        return pl.BlockSpec((16, SSD_XBC), lambda b, s: (jnp.maximum(_ssd_rowblock(b, s, direction) * 8 - 1, 0), 0))

    def next_spec(direction):
        return pl.BlockSpec((16, SSD_XBC), lambda b, s: (jnp.minimum(_ssd_rowblock(b, s, direction) * 8 + 8, n16 - 1), 0))

    def dt_spec(direction):
        return pl.BlockSpec((SSD_CHUNK, LANES), lambda b, s: (_ssd_rowblock(b, s, direction), 0))

    def y_spec(direction):
        return pl.BlockSpec((SSD_CHUNK, SSD_W), lambda b, s: (_ssd_rowblock(b, s, direction), 0))

    const = lambda shape: pl.BlockSpec(shape, lambda b, s: (0, 0))
    pad16 = lambda v: jnp.pad(v.reshape(1, 2 * SSD_H).astype(F32), ((0, 0), (0, LANES - 2 * SSD_H)))
    a_row = pad16(-jnp.exp(a_log.astype(F32)))
    bias_row = pad16(dt_bias)
    dskip_row = jnp.repeat(d_skip.astype(F32), SSD_W // SSD_H).reshape(1, SSD_W)
    return pl.pallas_call(
        _ssd_kernel,
        grid=(B, SSD_STEPS),
        in_specs=[chunk_spec(0), prev_spec(0), next_spec(0), dt_spec(0),
                  chunk_spec(1), prev_spec(1), next_spec(1), dt_spec(1),
                  const((3, SSD_XBC)), const((1, SSD_XBC)), const((1, LANES)), const((1, LANES)), const((1, SSD_W))],
        out_specs=(y_spec(0), y_spec(1)),
        out_shape=(jax.ShapeDtypeStruct((M, SSD_W), F32), jax.ShapeDtypeStruct((M, SSD_W), F32)),
        scratch_shapes=[pltpu.VMEM((2, 4, SSD_N, LANES), F32)],
        compiler_params=_params("arbitrary", "arbitrary"),
        name="ssd_scan",
    )(zx, zx, zx, dt, zx, zx, zx, dt, conv_w, conv_b.reshape(1, SSD_XBC), a_row, bias_row, dskip_row)


def _ssd_gate_kernel(yf_ref, yb_ref, z_ref, g_ref, o_ref):
    y = (yf_ref[...] + yb_ref[...]) * _silu(z_ref[...].astype(F32))
    y = y * lax.rsqrt(jnp.mean(y * y, axis=-1, keepdims=True) + LN_EPS)
    o_ref[...] = (y * g_ref[...]).astype(o_ref.dtype)


def ssd_gate(yf, yb, zx, norm_g):
    tm = 512
    row = lambda i: (i, 0)
    return pl.pallas_call(
        _ssd_gate_kernel,
        grid=(M // tm,),
        in_specs=[pl.BlockSpec((tm, SSD_W), row), pl.BlockSpec((tm, SSD_W), row),
                  pl.BlockSpec((tm, SSD_W), lambda i: (i, SSD_XBC // SSD_W)),
                  pl.BlockSpec((1, SSD_W), lambda i: (0, 0))],
        out_specs=pl.BlockSpec((tm, SSD_W), row),
        out_shape=jax.ShapeDtypeStruct((M, SSD_W), BF16),
        compiler_params=_params("arbitrary"),
        name="ssd_gate",
    )(yf, yb, zx, norm_g.reshape(1, SSD_W))


def _outproj_kernel(yh_ref, ys_ref, ya_ref, w_ref, x_ref, g1_ref, lng_ref, lnb_ref, sh2_ref, sc2_ref, r_ref,
                    x1_ref, h2_ref, aff_ref):
    y = jnp.dot(yh_ref[...], w_ref[0:HY_W, :], preferred_element_type=F32)
    y = y + jnp.dot(ys_ref[...], w_ref[HY_W:HY_W + SSD_W, :], preferred_element_type=F32)
    y = y + jnp.dot(ya_ref[...], w_ref[HY_W + SSD_W:, :], preferred_element_type=F32)
    x1 = _ln(ALPHA * x_ref[...] + g1_ref[...] * y) * lng_ref[...] + lnb_ref[...]
    x1_ref[...] = x1
    h2 = (_ln(x1) * (1.0 + sc2_ref[...]) + sh2_ref[...]).astype(BF16)
    h2_ref[...] = h2
    logits = jnp.dot(h2, r_ref[...], preferred_element_type=F32)
    lane = lax.broadcasted_iota(jnp.int32, logits.shape, 1)
    logits = jnp.where(lane < N_EXPERTS, logits, -jnp.inf)
    e = jnp.exp(logits - jnp.max(logits, axis=-1, keepdims=True))
    aff_ref[...] = e / jnp.sum(e, axis=-1, keepdims=True)


def out_projection(yh, ys, ya, w_out, x, mod_l, ln_g, ln_b, router):
    row = lambda i: (i, 0)
    const = lambda i: (0, 0)
    r_pad = jnp.pad(router, ((0, 0), (0, LANES - N_EXPERTS))).astype(BF16)
    return pl.pallas_call(
        _outproj_kernel,
        grid=(M // TM,),
        in_specs=[pl.BlockSpec((TM, HY_W), row), pl.BlockSpec((TM, SSD_W), row), pl.BlockSpec((TM, DA_W), row),
                  pl.BlockSpec((D, D), const), pl.BlockSpec((TM, D), row), _mod_spec(2),
                  pl.BlockSpec((1, D), const), pl.BlockSpec((1, D), const), _mod_spec(3), _mod_spec(4),
                  pl.BlockSpec((D, LANES), const)],
        out_specs=(pl.BlockSpec((TM, D), row), pl.BlockSpec((TM, D), row), pl.BlockSpec((TM, LANES), row)),
        out_shape=(jax.ShapeDtypeStruct((M, D), F32), jax.ShapeDtypeStruct((M, D), BF16),
                   jax.ShapeDtypeStruct((M, LANES), F32)),
        compiler_params=_params("arbitrary"),
        name="outproj",
    )(yh, ys, ya, w_out.astype(BF16), x, mod_l, ln_g.reshape(1, D), ln_b.reshape(1, D), mod_l, mod_l, r_pad)


CAP_LAT = EC_FACTOR * SEQ // N_EXPERTS
CAP_CTX = EC_FACTOR * CTX // N_EXPERTS
EROWS = B * (CAP_LAT + CAP_CTX)
FFN_TR = EROWS // 2
FFN_TF = 256


def _ffn_kernel(xs_ref, tw_ref, wg_ref, wu_ref, wd_ref, o_ref):
    f = pl.program_id(2)
    xs = xs_ref[...]
    g = jnp.dot(xs, wg_ref[...].astype(BF16), preferred_element_type=F32)
    u = jnp.dot(xs, wu_ref[...].astype(BF16), preferred_element_type=F32)
    t = jnp.dot((_silu(g) * u).astype(BF16), wd_ref[...].astype(BF16), preferred_element_type=F32)
    t = t * tw_ref[...]

    @pl.when(f == 0)
    def _():
        o_ref[...] = t

    @pl.when(f != 0)
    def _():
        o_ref[...] += t


def expert_ffn(xs, tw, w_gate, w_up, w_down):
    return pl.pallas_call(
        _ffn_kernel,
        grid=(N_EXPERTS, EROWS // FFN_TR, D_FF // FFN_TF),
        in_specs=[pl.BlockSpec((None, FFN_TR, D), lambda e, r, f: (e, r, 0)),
                  pl.BlockSpec((None, FFN_TR, 1), lambda e, r, f: (e, r, 0)),
                  pl.BlockSpec((None, D, FFN_TF), lambda e, r, f: (e, 0, f)),
                  pl.BlockSpec((None, D, FFN_TF), lambda e, r, f: (e, 0, f)),
                  pl.BlockSpec((None, FFN_TF, D), lambda e, r, f: (e, f, 0))],
        out_specs=pl.BlockSpec((None, FFN_TR, D), lambda e, r, f: (e, r, 0)),
        out_shape=jax.ShapeDtypeStruct((N_EXPERTS, EROWS, D), F32),
        compiler_params=_params("arbitrary", "arbitrary", "arbitrary"),
        name="expert_ffn",
    )(xs, tw, w_gate, w_up, w_down)


def _final_kernel(x_ref, m_ref, g2_ref, lng_ref, lnb_ref, o_ref):
    o_ref[...] = _ln(ALPHA * x_ref[...] + g2_ref[...] * m_ref[...]) * lng_ref[...] + lnb_ref[...]


def moe_residual(x1, moe, mod_l, ln_g, ln_b):
    row = lambda i: (i, 0)
    const = lambda i: (0, 0)
    return pl.pallas_call(
        _final_kernel,
        grid=(M // TM,),
        in_specs=[pl.BlockSpec((TM, D), row), pl.BlockSpec((TM, D), row), _mod_spec(5),
                  pl.BlockSpec((1, D), const), pl.BlockSpec((1, D), const)],
        out_specs=pl.BlockSpec((TM, D), row),
        out_shape=jax.ShapeDtypeStruct((M, D), F32),
        compiler_params=_params("arbitrary"),
        name="moe_residual",
    )(x1, moe, mod_l, ln_g.reshape(1, D), ln_b.reshape(1, D))


def _short_conv(u, w, b):
    y = lax.conv_general_dilated(u, w[:, None, :].astype(u.dtype), window_strides=(1,), padding='SAME',
                                 dimension_numbers=('NWC', 'WIO', 'NWC'), feature_group_count=u.shape[-1])
    return y + b.astype(u.dtype)


def _hyena_filters(L, w1, b1, w2, b2, w3, freq):
    t = jnp.arange(L, dtype=F32)
    t_unit = t / max(L - 1, 1)
    bands = jnp.linspace(1e-4, HY_BANDS - 1, HY_BANDS, dtype=F32)
    ang = (2.0 * math.pi / L) * t[:, None] * bands[None, :]
    feats = jnp.concatenate([t_unit[:, None], jnp.cos(ang), -jnp.sin(ang)], axis=-1)
    h = jnp.sin(freq * (jnp.dot(feats, w1, precision=HIGHEST) + b1))
    h = jnp.sin(freq * (jnp.dot(h, w2, precision=HIGHEST) + b2))
    h = jnp.dot(h, w3, precision=HIGHEST).reshape(L, 2, 2, HY_W)
    deltas = jnp.abs(jnp.linspace(HY_MIN_DECAY, HY_MAX_DECAY, HY_W, dtype=F32))
    h = h * jnp.exp(-t_unit[:, None, None, None] * deltas)
    fwd, bwd = h[:, 0], h[:, 1]
    k = jnp.concatenate([fwd, jnp.zeros_like(fwd[:1]), bwd[:0:-1]], axis=0)
    return k / jnp.sum(jnp.abs(k), axis=0, keepdims=True)


def _long_conv(z, k, skip):
    L = z.shape[1]
    zf = jnp.fft.rfft(z, n=2 * L, axis=1)
    kf = jnp.fft.rfft(k, axis=0)
    y = jnp.fft.irfft(zf * kf[None], n=2 * L, axis=1)[:, :L]
    return y + z * skip


def hyena_jnp(p, conv_w, conv_b, w1, b1, w2, b2, w3, freq, skip):
    L = p.shape[1]
    v, x1, x2 = jnp.split(_short_conv(p.astype(F32), conv_w, conv_b), 3, axis=-1)
    k = _hyena_filters(L, w1, b1, w2, b2, w3, freq)
    z = x1 * _long_conv(v, k[:, 0], skip[0])
    return x2 * _long_conv(z, k[:, 1], skip[1])


def route_jnp(aff, h2):
    def one(a, h, n, cap):
        a = a.reshape(B, n, N_EXPERTS)
        top_w, top_idx = lax.top_k(a.swapaxes(1, 2), cap)
        xs = jax.vmap(lambda hb, ib: hb[ib])(h.reshape(B, n, D), top_idx)
        return top_w, top_idx, xs.swapaxes(0, 1).reshape(N_EXPERTS, B * cap, D)

    wl, il, xl = one(aff[:NL, :N_EXPERTS], h2[:NL], SEQ, CAP_LAT)
    wc, ic, xc = one(aff[NL:, :N_EXPERTS], h2[NL:], CTX, CAP_CTX)
    xs = jnp.concatenate([xl, xc], axis=1)
    tw = jnp.concatenate([wl.swapaxes(0, 1).reshape(N_EXPERTS, -1), wc.swapaxes(0, 1).reshape(N_EXPERTS, -1)], axis=1)
    return xs, tw[..., None], il, ic


def combine_jnp(out, il, ic):
    def one(o, idx, n, cap):
        o = o.reshape(N_EXPERTS, B, cap, D).swapaxes(0, 1)
        f = lambda ib, ob: jnp.zeros((n, D), ob.dtype).at[ib.reshape(-1)].add(ob.reshape(-1, D))
        return jax.vmap(f)(idx, o).reshape(B * n, D)

    return jnp.concatenate([one(out[:, :B * CAP_LAT], il, SEQ, CAP_LAT), one(out[:, B * CAP_LAT:], ic, CTX, CAP_CTX)], axis=0)


def kernel(x, c, ctx, c_ctx, w_mod, b_mod, w_in, hy_conv_w, hy_conv_b, hy_w1, hy_b1, hy_w2, hy_b2, hy_w3, hy_freq, hy_skip, ssd_conv_w, ssd_conv_b, ssd_a_log, ssd_dt_bias, ssd_d, ssd_norm_g, da_lambda, da_subln_g, w_out, ln1_g, ln1_b, router, w_gate, w_up, w_down, ln2_g, ln2_b):
    xa = jnp.concatenate([x.reshape(NL, D), ctx.reshape(NC, D)], axis=0)
    cc = jnp.concatenate([c, c_ctx[None, :], jnp.zeros((8 - B - 1, D), F32)], axis=0)
    mod = modulation(cc, w_mod, b_mod).reshape(DEPTH, 8, 1, 6 * D)
    rope = rope_tables()
    o_ssd = HY_PROJ
    o_xbc = o_ssd + SSD_W
    o_dt = o_xbc + SSD_XBC
    o_da = o_dt + 2 * SSD_H
    for l in range(DEPTH):
        lam_init = 0.8 - 0.6 * math.exp(-0.3 * l)
        w = w_in[l]
        w_hy = w[:, :o_ssd].astype(BF16)
        w_ssd = jnp.concatenate([w[:, o_xbc:o_dt], w[:, o_ssd:o_xbc], w[:, o_dt:o_da],
                                 jnp.zeros((D, LANES - 2 * SSD_H), F32)], axis=1).astype(BF16)
        w_da = w[:, o_da:].astype(BF16)
        p_hy = in_projection(xa, mod[l], w_hy, "hy")
        zx, dt = in_projection(xa, mod[l], w_ssd, "ssd")
        da = in_projection(xa, mod[l], w_da, "da", rope)

        hy_args = (hy_conv_w[l], hy_conv_b[l], hy_w1[l], hy_b1[l], hy_w2[l], hy_b2[l], hy_w3[l], hy_freq[l], hy_skip[l])
        y_hy = jnp.concatenate([hyena_jnp(p_hy[:NL].reshape(B, SEQ, HY_PROJ), *hy_args).reshape(NL, HY_W),
                                hyena_jnp(p_hy[NL:].reshape(B, CTX, HY_PROJ), *hy_args).reshape(NC, HY_W)],
                               axis=0).astype(BF16)

        yf, yb = ssd_scan(zx, dt, ssd_conv_w[l], ssd_conv_b[l], ssd_a_log[l], ssd_dt_bias[l], ssd_d[l])
        y_ssd = ssd_gate(yf, yb, zx, ssd_norm_g[l])

        lp = da_lambda[l].astype(F32)
        lam = jnp.exp(jnp.sum(lp[0] * lp[1])) - jnp.exp(jnp.sum(lp[2] * lp[3])) + lam_init
        y_da = jnp.concatenate([diff_attention(da, lam, da_subln_g[l], lam_init, False),
                                diff_attention(da, lam, da_subln_g[l], lam_init, True)], axis=0)

        x1, h2, aff = out_projection(y_hy, y_ssd, y_da, w_out[l], xa, mod[l], ln1_g[l], ln1_b[l], router[l])
        xs, tw, il, ic = route_jnp(aff, h2)
        eo = expert_ffn(xs, tw, w_gate[l], w_up[l], w_down[l])
        moe = combine_jnp(eo, il, ic)
        xa = moe_residual(x1, moe, mod[l], ln2_g[l], ln2_b[l])
    return xa[:NL].reshape(B, SEQ, D)
```

```python
import functools
import math

import numpy as np
import jax
import jax.numpy as jnp
from jax import lax
from jax.experimental import pallas as pl
from jax.experimental.pallas import tpu as pltpu

F32 = jnp.float32
BF16 = jnp.bfloat16
HIGHEST = lax.Precision.HIGHEST

D = 2048
B = 4
SEQ = 4096
DEPTH = 4
GRID_W = 64
CTX = 256
HY_W = 512
SSD_W = 512
DA_W = 1024
HY_EMB = 33
HY_BANDS = 16
HY_HID = 64
HY_MIN_DECAY = math.log(1e-2) / 1.5
HY_MAX_DECAY = math.log(1e-2) / 0.3
HY_PROJ = 3 * HY_W
SSD_H = 8
SSD_N = 128
SSD_CHUNK = 128
SSD_XBC = 1024
DA_H = 8
DA_DK = 64
DA_QK = 1024
ROPE_THETA = 10000.0
ROPE_PAIRS = 16
N_EXPERTS = 16
EC_FACTOR = 2
D_FF = 1536
ALPHA = (2 * DEPTH) ** 0.25
LN_EPS = 1e-6

NL = B * SEQ
NC = B * CTX
M = NL + NC
LANES = 128
TM = 256
VMEM_LIMIT = 56 * 1024 * 1024

N_LAT_TILES = NL // TM
TILES_PER_SAMPLE = SEQ // TM


def _mod_row(i):
    return jnp.where(i < N_LAT_TILES, i // TILES_PER_SAMPLE, B)


def _mod_spec(chunk):
    return pl.BlockSpec((None, 1, D), lambda i, chunk=chunk: (_mod_row(i), 0, chunk))


def _params(*sem):
    return pltpu.CompilerParams(dimension_semantics=sem, vmem_limit_bytes=VMEM_LIMIT)


def _ln(x):
    mu = jnp.mean(x, axis=-1, keepdims=True)
    xc = x - mu
    var = jnp.mean(xc * xc, axis=-1, keepdims=True)
    return xc * lax.rsqrt(var + LN_EPS)


def _silu(x):
    return x * jax.nn.sigmoid(x)


def _mod_kernel(c_ref, w_ref, b_ref, o_ref):
    s = _silu(c_ref[...])
    o_ref[...] = jnp.dot(s, w_ref[...], precision=HIGHEST, preferred_element_type=F32) + b_ref[...]


def modulation(cc, w_mod, b_mod):
    tn = 1024
    return pl.pallas_call(
        _mod_kernel,
        grid=(DEPTH, 6 * D // tn),
        in_specs=[
            pl.BlockSpec((8, D), lambda l, j: (0, 0)),
            pl.BlockSpec((None, D, tn), lambda l, j: (l, 0, j)),
            pl.BlockSpec((None, 1, tn), lambda l, j: (l, 0, j)),
        ],
        out_specs=pl.BlockSpec((None, 8, tn), lambda l, j: (l, 0, j)),
        out_shape=jax.ShapeDtypeStruct((DEPTH, 8, 6 * D), F32),
        compiler_params=_params("arbitrary", "arbitrary"),
        name="modulation",
    )(cc, w_mod, b_mod.reshape(DEPTH, 1, 6 * D))


def _inproj_kernel(x_ref, sh_ref, sc_ref, w_ref, *rest, mode):
    h = _ln(x_ref[...]) * (1.0 + sc_ref[...]) + sh_ref[...]
    acc = jnp.dot(h.astype(BF16), w_ref[...], preferred_element_type=F32)
    if mode == "hy":
        (o_ref,) = rest
        o_ref[...] = acc.astype(o_ref.dtype)
    elif mode == "ssd":
        o_ref, dt_ref = rest
        o_ref[...] = acc[:, : SSD_XBC + SSD_W].astype(o_ref.dtype)
        dt_ref[...] = acc[:, SSD_XBC + SSD_W:]
    else:
        cos_ref, sin_ref, o_ref = rest
        cs = cos_ref[...]
        sn = sin_ref[...]
        lane = lax.broadcasted_iota(jnp.int32, (TM, LANES), 1)
        is_a = (lane % 32) < ROPE_PAIRS
        for g in range(2 * DA_QK // LANES):
            blk = acc[:, g * LANES:(g + 1) * LANES]
            swapped = jnp.where(is_a, pltpu.roll(blk, LANES - ROPE_PAIRS, 1), pltpu.roll(blk, ROPE_PAIRS, 1))
            rot = blk * cs + swapped * sn
            if g < DA_QK // LANES:
                rot = rot * (DA_DK ** -0.5 * math.log2(math.e))
            o_ref[:, g * LANES:(g + 1) * LANES] = rot.astype(o_ref.dtype)
        o_ref[:, 2 * DA_QK:] = acc[:, 2 * DA_QK:].astype(o_ref.dtype)


def in_projection(x, mod_l, w, l, mode, rope=None):
    n = w.shape[2]
    in_specs = [
        pl.BlockSpec((TM, D), lambda i: (i, 0)),
        _mod_spec(0),
        _mod_spec(1),
        pl.BlockSpec((None, D, n), lambda i: (l, 0, 0)),
    ]
    args = [x, mod_l, mod_l, w]
    if mode == "hy":
        out_shape = jax.ShapeDtypeStruct((M, n), BF16)
        out_specs = pl.BlockSpec((TM, n), lambda i: (i, 0))
    elif mode == "ssd":
        n0 = SSD_XBC + SSD_W
        out_shape = (jax.ShapeDtypeStruct((M, n0), BF16), jax.ShapeDtypeStruct((M, LANES), F32))
        out_specs = (pl.BlockSpec((TM, n0), lambda i: (i, 0)), pl.BlockSpec((TM, LANES), lambda i: (i, 0)))
    else:
        rope_idx = lambda i: (jnp.where(i < N_LAT_TILES, i % TILES_PER_SAMPLE, TILES_PER_SAMPLE), 0)
        in_specs += [pl.BlockSpec((TM, LANES), rope_idx), pl.BlockSpec((TM, LANES), rope_idx)]
        args += list(rope)
        out_shape = jax.ShapeDtypeStruct((M, n), BF16)
        out_specs = pl.BlockSpec((TM, n), lambda i: (i, 0))
    return pl.pallas_call(
        functools.partial(_inproj_kernel, mode=mode),
        grid=(M // TM,),
        in_specs=in_specs,
        out_specs=out_specs,
        out_shape=out_shape,
        compiler_params=_params("arbitrary"),
        name="inproj_" + mode,
    )(*args)


def rope_tables():
    pos = np.arange(SEQ)
    r = (pos // GRID_W).astype(np.float64)
    col = (pos % GRID_W).astype(np.float64)
    inv = ROPE_THETA ** (-np.arange(ROPE_PAIRS, dtype=np.float64) / ROPE_PAIRS)
    inv = inv.astype(np.float32).astype(np.float64)
    lane = np.arange(LANES)
    d = lane % DA_DK
    axis = d // 32
    pair = d % ROPE_PAIRS
    is_b = (d % 32) >= ROPE_PAIRS
    ang = np.where(axis[None, :] == 0, r[:, None], col[:, None]) * inv[pair][None, :]
    cos = np.cos(ang)
    sin = np.where(is_b[None, :], np.sin(ang), -np.sin(ang))
    cos = np.concatenate([cos, np.ones((TM, LANES))], axis=0)
    sin = np.concatenate([sin, np.zeros((TM, LANES))], axis=0)
    return jnp.asarray(cos, F32), jnp.asarray(sin, F32)


def _attn_kernel(lam_ref, q_ref, *rest, n_seg, post_scale):
    kv = rest[: 2 * n_seg]
    g_ref, o_ref = rest[2 * n_seg:]
    lam = lam_ref[0]
    q = q_ref[...]
    lane = lax.broadcasted_iota(jnp.int32, q.shape, 1)
    zero = jnp.zeros_like(q)
    qm = (jnp.where(lane < DA_DK, q, zero), jnp.where(lane >= DA_DK, q, zero))
    nt = (((1,), (1,)), ((), ()))
    scores = [[lax.dot_general(qm[m], kv[2 * s][...], nt, preferred_element_type=F32) for s in range(n_seg)]
              for m in range(2)]
    probs = []
    for m in range(2):
        mx = functools.reduce(jnp.maximum, [jnp.max(s, axis=-1, keepdims=True) for s in scores[m]])
        es = [jnp.exp2(s - mx) for s in scores[m]]
        tot = functools.reduce(lambda a, b: a + b, [jnp.sum(e, axis=-1, keepdims=True) for e in es])
        probs.append((es, 1.0 / tot))
    out = None
    for s in range(n_seg):
        a = probs[0][0][s] * probs[0][1] - (lam * probs[1][1]) * probs[1][0][s]
        o = jnp.dot(a.astype(BF16), kv[2 * s + 1][...], preferred_element_type=F32)
        out = o if out is None else out + o
    out = out * lax.rsqrt(jnp.mean(out * out, axis=-1, keepdims=True) + LN_EPS)
    o_ref[...] = ((out * g_ref[...]) * post_scale).astype(o_ref.dtype)


def diff_attention(da, lam, subln_g, lam_init, ctx_queries):
    tq = 256
    lam = jnp.reshape(lam, (1,)).astype(F32)
    g = subln_g.reshape(1, LANES)
    kc_spec = pl.BlockSpec((CTX, LANES), lambda b, h, i: (NL // CTX + b, DA_H + h))
    vc_spec = pl.BlockSpec((CTX, LANES), lambda b, h, i: (NL // CTX + b, 2 * DA_H + h))
    if ctx_queries:
        rows, nq = NC, CTX // tq
        q_spec = pl.BlockSpec((tq, LANES), lambda b, h, i: ((NL + b * CTX) // tq + i, h))
        segs, seg_args = [kc_spec, vc_spec], [da, da]
    else:
        rows, nq = NL, SEQ // tq
        q_spec = pl.BlockSpec((tq, LANES), lambda b, h, i: (b * nq + i, h))
        kl_spec = pl.BlockSpec((SEQ, LANES), lambda b, h, i: (b, DA_H + h))
        vl_spec = pl.BlockSpec((SEQ, LANES), lambda b, h, i: (b, 2 * DA_H + h))
        segs, seg_args = [kl_spec, vl_spec, kc_spec, vc_spec], [da, da, da, da]
    return pl.pallas_call(
        functools.partial(_attn_kernel, n_seg=len(segs) // 2, post_scale=1.0 - lam_init),
        grid=(B, DA_H, nq),
        in_specs=[pl.BlockSpec(memory_space=pltpu.SMEM), q_spec] + segs + [pl.BlockSpec((1, LANES), lambda b, h, i: (0, 0))],
        out_specs=pl.BlockSpec((tq, LANES), lambda b, h, i: (b * nq + i, h)),
        out_shape=jax.ShapeDtypeStruct((rows, DA_W), BF16),
        compiler_params=_params("arbitrary", "arbitrary", "arbitrary"),
        name="diff_attn_ctx" if ctx_queries else "diff_attn_lat",
    )(lam, da, *seg_args, g)


SSD_NC_CTX = CTX // SSD_CHUNK
SSD_NC_LAT = SEQ // SSD_CHUNK
SSD_STEPS = SSD_NC_CTX + SSD_NC_LAT


def _ssd_chunk(s, direction):
    in_ctx = s < SSD_NC_CTX
    k = jnp.where(in_ctx, s, s - SSD_NC_CTX)
    nck = jnp.where(in_ctx, SSD_NC_CTX, SSD_NC_LAT)
    if direction == 1:
        k = nck - 1 - k
    return in_ctx, k, nck


def _ssd_rowblock(b, s, direction):
    in_ctx, k, _ = _ssd_chunk(s, direction)
    return jnp.where(in_ctx, NL // SSD_CHUNK + b * SSD_NC_CTX + k, b * SSD_NC_LAT + k)


def _ssd_kernel(xf_ref, xfp_ref, xfn_ref, dtf_ref, xb_ref, xbp_ref, xbn_ref, dtb_ref,
                cw_ref, cb_ref, a_ref, dtbias_ref, dskip_ref, yf_ref, yb_ref, state_ref):
    s = pl.program_id(1)
    q = SSD_CHUNK

    @pl.when(s == 0)
    def _():
        state_ref[...] = jnp.zeros_like(state_ref)

    row = lax.broadcasted_iota(jnp.int32, (q, 1), 0)
    ri = lax.broadcasted_iota(jnp.int32, (q, q), 0)
    ci = lax.broadcasted_iota(jnp.int32, (q, q), 1)
    lane_lo = lax.broadcasted_iota(jnp.int32, (1, LANES), 1) < 64
    cw = cw_ref[...]
    dirs = ((xf_ref, xfp_ref, xfn_ref, dtf_ref, yf_ref), (xb_ref, xbp_ref, xbn_ref, dtb_ref, yb_ref))
    for d, (x_ref, xp_ref, xn_ref, dt_ref, y_ref) in enumerate(dirs):
        _, k, nck = _ssd_chunk(s, d)
        u = x_ref[...].astype(F32)
        prev = jnp.where(k == 0, 0.0, xp_ref[15:16, :].astype(F32))
        nxt = jnp.where(k == nck - 1, 0.0, xn_ref[0:1, :].astype(F32))
        up = jnp.where(row == 0, prev, pltpu.roll(u, 1, 0))
        dn = jnp.where(row == q - 1, nxt, pltpu.roll(u, q - 1, 0))
        conv = cw[0:1, :] * up + cw[1:2, :] * u + cw[2:3, :] * dn + cb_ref[...]
        xbc = _silu(conv)
        xs = xbc[:, :SSD_W]
        bm = xbc[:, SSD_W:SSD_W + 2 * SSD_N]
        cm = xbc[:, SSD_W + 2 * SSD_N:]
        z = dt_ref[...] + dtbias_ref[...]
        dt = jnp.maximum(z, 0.0) + jnp.log(1.0 + jnp.exp(-jnp.abs(z)))
        a = dt * a_ref[...]
        tri = (ci <= ri) if d == 0 else (ci >= ri)
        a_cum = jnp.dot(tri.astype(F32), a, precision=HIGHEST, preferred_element_type=F32)
        a_cum_t = a_cum.T
        ea = jnp.exp(a_cum)
        a_tot = a_cum[q - 1:q, :] if d == 0 else a_cum[0:1, :]
        to_end = jnp.exp(a_tot - a_cum)
        chunk_decay = jnp.exp(a_tot)

        def halves(col_arr, c0, c1):
            return jnp.where(lane_lo, col_arr[:, c0:c0 + 1], col_arr[:, c1:c1 + 1])

        for g in range(2):
            bg = bm[:, g * SSD_N:(g + 1) * SSD_N]
            cgb = cm[:, g * SSD_N:(g + 1) * SSD_N].astype(BF16)
            gmat = lax.dot_general(cgb, bg.astype(BF16), (((1,), (1,)), ((), ())), preferred_element_type=F32)
            bgt = bg.T.astype(BF16)
            for pr in range(2):
                p = 2 * g + pr
                c0, c1 = 8 * d + 2 * p, 8 * d + 2 * p + 1
                xpair = xs[:, p * LANES:(p + 1) * LANES]
                xdt = xpair * halves(dt, c0, c1)
                y = None
                for hh, hc in ((0, c0), (1, c1)):
                    seg = a_cum[:, hc:hc + 1] - a_cum_t[hc:hc + 1, :]
                    lmat = jnp.exp(jnp.where(tri, seg, -jnp.inf))
                    xm = jnp.where(lane_lo if hh == 0 else jnp.logical_not(lane_lo), xdt, 0.0)
                    t = jnp.dot((gmat * lmat).astype(BF16), xm.astype(BF16), preferred_element_type=F32)
                    y = t if y is None else y + t
                st = state_ref[d, p]
                y = y + halves(ea, c0, c1) * jnp.dot(cgb, st.astype(BF16), preferred_element_type=F32)
                upd = jnp.dot(bgt, (xdt * halves(to_end, c0, c1)).astype(BF16), preferred_element_type=F32)
                state_ref[d, p] = halves(chunk_decay, c0, c1) * st + upd
                if d == 0:
                    y = y + dskip_ref[:, p * LANES:(p + 1) * LANES] * xpair
                y_ref[:, p * LANES:(p + 1) * LANES] = y


def ssd_scan(zx, dt, conv_w, conv_b, a_log, dt_bias, d_skip):
    n16 = M // 16

    def chunk_spec(direction):
        return pl.BlockSpec((SSD_CHUNK, SSD_XBC), lambda b, s: (_ssd_rowblock(b, s, direction), 0))

    def prev_spec(direction):
        return pl.BlockSpec((16, SSD_XBC), lambda b, s: (jnp.maximum(_ssd_rowblock(b, s, direction) * 8 - 1, 0), 0))

    def next_spec(direction):
        return pl.BlockSpec((16, SSD_XBC), lambda b, s: (jnp.minimum(_ssd_rowblock(b, s, direction) * 8 + 8, n16 - 1), 0))

    def dt_spec(direction):
        return pl.BlockSpec((SSD_CHUNK, LANES), lambda b, s: (_ssd_rowblock(b, s, direction), 0))

    def y_spec(direction):
        return pl.BlockSpec((SSD_CHUNK, SSD_W), lambda b, s: (_ssd_rowblock(b, s, direction), 0))

    const = lambda shape: pl.BlockSpec(shape, lambda b, s: (0, 0))
    pad16 = lambda v: jnp.pad(v.reshape(1, 2 * SSD_H).astype(F32), ((0, 0), (0, LANES - 2 * SSD_H)))
    a_row = pad16(-jnp.exp(a_log.astype(F32)))
    bias_row = pad16(dt_bias)
    dskip_row = jnp.repeat(d_skip.astype(F32), SSD_W // SSD_H).reshape(1, SSD_W)
    return pl.pallas_call(
        _ssd_kernel,
        grid=(B, SSD_STEPS),
        in_specs=[chunk_spec(0), prev_spec(0), next_spec(0), dt_spec(0),
                  chunk_spec(1), prev_spec(1), next_spec(1), dt_spec(1),
                  const((3, SSD_XBC)), const((1, SSD_XBC)), const((1, LANES)), const((1, LANES)), const((1, SSD_W))],
        out_specs=(y_spec(0), y_spec(1)),
        out_shape=(jax.ShapeDtypeStruct((M, SSD_W), F32), jax.ShapeDtypeStruct((M, SSD_W), F32)),
        scratch_shapes=[pltpu.VMEM((2, 4, SSD_N, LANES), F32)],
        compiler_params=_params("arbitrary", "arbitrary"),
        name="ssd_scan",
    )(zx, zx, zx, dt, zx, zx, zx, dt, conv_w, conv_b.reshape(1, SSD_XBC), a_row, bias_row, dskip_row)


def _ssd_gate_kernel(yf_ref, yb_ref, z_ref, g_ref, o_ref):
    y = (yf_ref[...] + yb_ref[...]) * _silu(z_ref[...].astype(F32))
    y = y * lax.rsqrt(jnp.mean(y * y, axis=-1, keepdims=True) + LN_EPS)
    o_ref[...] = (y * g_ref[...]).astype(o_ref.dtype)


def ssd_gate(yf, yb, zx, norm_g):
    tm = 512
    row = lambda i: (i, 0)
    return pl.pallas_call(
        _ssd_gate_kernel,
        grid=(M // tm,),
        in_specs=[pl.BlockSpec((tm, SSD_W), row), pl.BlockSpec((tm, SSD_W), row),
                  pl.BlockSpec((tm, SSD_W), lambda i: (i, SSD_XBC // SSD_W)),
                  pl.BlockSpec((1, SSD_W), lambda i: (0, 0))],
        out_specs=pl.BlockSpec((tm, SSD_W), row),
        out_shape=jax.ShapeDtypeStruct((M, SSD_W), BF16),
        compiler_params=_params("arbitrary"),
        name="ssd_gate",
    )(yf, yb, zx, norm_g.reshape(1, SSD_W))


def _outproj_kernel(yh_ref, ys_ref, ya_ref, w_ref, x_ref, g1_ref, lng_ref, lnb_ref, sh2_ref, sc2_ref, r_ref,
                    x1_ref, h2_ref, aff_ref):
    y = jnp.dot(yh_ref[...], w_ref[0:HY_W, :], preferred_element_type=F32)
    y = y + jnp.dot(ys_ref[...], w_ref[HY_W:HY_W + SSD_W, :], preferred_element_type=F32)
    y = y + jnp.dot(ya_ref[...], w_ref[HY_W + SSD_W:, :], preferred_element_type=F32)
    x1 = _ln(ALPHA * x_ref[...] + g1_ref[...] * y) * lng_ref[...] + lnb_ref[...]
    x1_ref[...] = x1
    h2 = (_ln(x1) * (1.0 + sc2_ref[...]) + sh2_ref[...]).astype(BF16)
    h2_ref[...] = h2
    logits = jnp.dot(h2, r_ref[...], preferred_element_type=F32)
    lane = lax.broadcasted_iota(jnp.int32, logits.shape, 1)
    logits = jnp.where(lane < N_EXPERTS, logits, -jnp.inf)
    e = jnp.exp(logits - jnp.max(logits, axis=-1, keepdims=True))
    aff_ref[...] = e / jnp.sum(e, axis=-1, keepdims=True)


def out_projection(yh, ys, ya, w_out, l, x, mod_l, ln_g, ln_b, r_pad):
    row = lambda i: (i, 0)
    const = lambda i: (0, 0)
    layer = lambda i: (l, 0, 0)
    return pl.pallas_call(
        _outproj_kernel,
        grid=(M // TM,),
        in_specs=[pl.BlockSpec((TM, HY_W), row), pl.BlockSpec((TM, SSD_W), row), pl.BlockSpec((TM, DA_W), row),
                  pl.BlockSpec((None, D, D), layer), pl.BlockSpec((TM, D), row), _mod_spec(2),
                  pl.BlockSpec((1, D), const), pl.BlockSpec((1, D), const), _mod_spec(3), _mod_spec(4),
                  pl.BlockSpec((None, D, LANES), layer)],
        out_specs=(pl.BlockSpec((TM, D), row), pl.BlockSpec((TM, D), row), pl.BlockSpec((TM, LANES), row)),
        out_shape=(jax.ShapeDtypeStruct((M, D), F32), jax.ShapeDtypeStruct((M, D), BF16),
                   jax.ShapeDtypeStruct((M, LANES), F32)),
        compiler_params=_params("arbitrary"),
        name="outproj",
    )(yh, ys, ya, w_out, x, mod_l, ln_g.reshape(1, D), ln_b.reshape(1, D), mod_l, mod_l, r_pad)


CAP_LAT = EC_FACTOR * SEQ // N_EXPERTS
CAP_CTX = EC_FACTOR * CTX // N_EXPERTS
EROWS = B * (CAP_LAT + CAP_CTX)
FFN_TR = EROWS // 2
FFN_TF = 256


def _ffn_kernel(xs_ref, tw_ref, wg_ref, wu_ref, wd_ref, o_ref):
    f = pl.program_id(2)
    xs = xs_ref[...]
    g = jnp.dot(xs, wg_ref[...].astype(BF16), preferred_element_type=F32)
    u = jnp.dot(xs, wu_ref[...].astype(BF16), preferred_element_type=F32)
    t = jnp.dot((_silu(g) * u).astype(BF16), wd_ref[...].astype(BF16), preferred_element_type=F32)
    t = t * tw_ref[...]

    @pl.when(f == 0)
    def _():
        o_ref[...] = t

    @pl.when(f != 0)
    def _():
        o_ref[...] += t


def expert_ffn(xs, tw, w_gate, w_up, w_down, l):
    return pl.pallas_call(
        _ffn_kernel,
        grid=(N_EXPERTS, EROWS // FFN_TR, D_FF // FFN_TF),
        in_specs=[pl.BlockSpec((None, FFN_TR, D), lambda e, r, f: (e, r, 0)),
                  pl.BlockSpec((None, FFN_TR, 1), lambda e, r, f: (e, r, 0)),
                  pl.BlockSpec((None, None, D, FFN_TF), lambda e, r, f: (l, e, 0, f)),
                  pl.BlockSpec((None, None, D, FFN_TF), lambda e, r, f: (l, e, 0, f)),
                  pl.BlockSpec((None, None, FFN_TF, D), lambda e, r, f: (l, e, f, 0))],
        out_specs=pl.BlockSpec((None, FFN_TR, D), lambda e, r, f: (e, r, 0)),
        out_shape=jax.ShapeDtypeStruct((N_EXPERTS, EROWS, D), F32),
        compiler_params=_params("arbitrary", "arbitrary", "arbitrary"),
        name="expert_ffn",
    )(xs, tw, w_gate, w_up, w_down)


def _final_kernel(x_ref, m_ref, g2_ref, lng_ref, lnb_ref, o_ref):
    o_ref[...] = _ln(ALPHA * x_ref[...] + g2_ref[...] * m_ref[...]) * lng_ref[...] + lnb_ref[...]


def moe_residual(x1, moe, mod_l, ln_g, ln_b):
    row = lambda i: (i, 0)
    const = lambda i: (0, 0)
    return pl.pallas_call(
        _final_kernel,
        grid=(M // TM,),
        in_specs=[pl.BlockSpec((TM, D), row), pl.BlockSpec((TM, D), row), _mod_spec(5),
                  pl.BlockSpec((1, D), const), pl.BlockSpec((1, D), const)],
        out_specs=pl.BlockSpec((TM, D), row),
        out_shape=jax.ShapeDtypeStruct((M, D), F32),
        compiler_params=_params("arbitrary"),
        name="moe_residual",
    )(x1, moe, mod_l, ln_g.reshape(1, D), ln_b.reshape(1, D))


HY_R = 256
HY_TC = 256


def _hy_na(L):
    return 2 * L // HY_R


def _hy_tables(L):
    na = _hy_na(L)
    nk = na // 2 + 1
    eye = np.eye(8)
    ka = np.arange(nk)

    def s1(n_a):
        ang = -2.0 * np.pi * np.outer(ka, np.arange(n_a)) / na
        return np.concatenate([np.kron(np.cos(ang), eye), np.kron(np.sin(ang), eye)], axis=0)

    r = np.arange(HY_R)
    ang = -2.0 * np.pi * np.outer(r, r) / HY_R
    fs = np.concatenate([np.cos(ang), np.sin(ang)], axis=0)
    tang = -2.0 * np.pi * ka[:, None] * r[None, :] / (2 * L)
    tw_r = np.repeat(np.cos(tang)[:, :, None], LANES, axis=2)
    tw_i = np.repeat(np.sin(tang)[:, :, None], LANES, axis=2)
    wgt = np.where((ka == 0) | (ka == na // 2), 1.0, 2.0)
    iang = 2.0 * np.pi * np.outer(np.arange(na // 2), ka) / na
    g = np.concatenate([np.kron(np.cos(iang) * wgt, eye), np.kron(-np.sin(iang) * wgt, eye)], axis=1)
    f = lambda a: jnp.asarray(a, F32)
    return dict(s1_half=f(s1(na // 2)), s1_full=f(s1(na)), fs=f(fs), tw_r=f(tw_r), tw_i=f(tw_i), g=f(g))


def _hy_stage1(x_ref, s1, ar_ref, ai_ref, n_a, nk, prec):
    tc = x_ref.shape[-1]

    def body(rh, carry):
        rows = pl.ds(pl.multiple_of(rh * 8, 8), 8)
        slab = x_ref[:, rows, :].reshape(8 * n_a, tc)
        if prec is None:
            slab = slab.astype(BF16)
        out = jnp.dot(s1, slab, precision=prec, preferred_element_type=F32)
        ar_ref[:, rows, :] = out[: 8 * nk].reshape(nk, 8, tc)
        ai_ref[:, rows, :] = out[8 * nk:].reshape(nk, 8, tc)
        return carry

    lax.fori_loop(0, HY_R // 8, body, 0)


def _hy_cmul_tw(xr, xi, twr, twi, conj):
    reps = xr.shape[-1] // LANES
    if reps > 1:
        twr = jnp.concatenate([twr] * reps, axis=1)
        twi = jnp.concatenate([twi] * reps, axis=1)
    if conj:
        return xr * twr + xi * twi, xi * twr - xr * twi
    return xr * twr - xi * twi, xi * twr + xr * twi


def _hy_dft256(fs, xr, xi, inverse, prec):
    if prec is None:
        xr, xi = xr.astype(BF16), xi.astype(BF16)
    t1 = jnp.dot(fs, xr, precision=prec, preferred_element_type=F32)
    t2 = jnp.dot(fs, xi, precision=prec, preferred_element_type=F32)
    if inverse:
        return t1[:HY_R] + t2[HY_R:], t2[:HY_R] - t1[HY_R:]
    return t1[:HY_R] - t2[HY_R:], t2[:HY_R] + t1[HY_R:]


def _hy_filter_kernel(w1t_ref, w1c_ref, w1s_ref, b1_ref, w2_ref, b2_ref, w3_ref, freq_ref, dl_ref, o_ref, norm_ref, *, L):
    i = pl.program_id(0)
    tn = o_ref.shape[0]
    n = i * tn + lax.broadcasted_iota(jnp.int32, (tn, 1), 0)
    t = jnp.where(n < L, n, 2 * L - n).astype(F32)
    t_unit = t / max(L - 1, 1)
    j = lax.broadcasted_iota(jnp.int32, (1, HY_BANDS), 1).astype(F32)
    bands = 1e-4 + j * ((HY_BANDS - 1 - 1e-4) / (HY_BANDS - 1))
    ang = (2.0 * math.pi / L) * t * bands
    f = freq_ref[...]
    dot = functools.partial(jnp.dot, precision=HIGHEST, preferred_element_type=F32)
    h = t_unit * w1t_ref[...] + dot(jnp.cos(ang), w1c_ref[...]) - dot(jnp.sin(ang), w1s_ref[...])
    h = jnp.sin(f * (h + b1_ref[...]))
    h = jnp.sin(f * (dot(h, w2_ref[...]) + b2_ref[...]))
    hh = dot(h, w3_ref[...])
    half = 2 * HY_W
    k = jnp.where(n < L, hh[:, :half], hh[:, half:]) * jnp.exp(-t_unit * dl_ref[...])
    k = jnp.where(n == L, 0.0, k)
    o_ref[...] = k

    @pl.when(i == 0)
    def _():
        norm_ref[...] = jnp.zeros_like(norm_ref)

    norm_ref[...] += jnp.sum(jnp.abs(k), axis=0, keepdims=True)


def hyena_filter_time(L, w1, b1, w2, b2, w3, freq):
    tn = min(512, 2 * L)
    deltas = np.abs(np.linspace(HY_MIN_DECAY, HY_MAX_DECAY, HY_W, dtype=np.float32))
    dl = jnp.asarray(np.tile(deltas, 2)[None, :], F32)
    const = lambda shape: pl.BlockSpec(shape, lambda i: (0, 0))
    return pl.pallas_call(
        functools.partial(_hy_filter_kernel, L=L),
        grid=(2 * L // tn,),
        in_specs=[const((1, HY_HID)), const((HY_BANDS, HY_HID)), const((HY_BANDS, HY_HID)), const((1, HY_HID)),
                  const((HY_HID, HY_HID)), const((1, HY_HID)), const((HY_HID, 4 * HY_W)), const((1, HY_HID)),
                  const((1, 2 * HY_W))],
        out_specs=(pl.BlockSpec((tn, 2 * HY_W), lambda i: (i, 0)), const((1, 2 * HY_W))),
        out_shape=(jax.ShapeDtypeStruct((2 * L, 2 * HY_W), F32), jax.ShapeDtypeStruct((1, 2 * HY_W), F32)),
        compiler_params=_params("arbitrary"),
        name="hyena_filter_time",
    )(w1[0:1], w1[1:1 + HY_BANDS], w1[1 + HY_BANDS:], b1.reshape(1, -1), w2, b2.reshape(1, -1), w3,
      freq.reshape(1, -1), dl)


def _hy_spectrum_kernel(k_ref, norm_ref, s1_ref, fs_ref, twr_ref, twi_ref, kr_ref, ki_ref, ar_s, ai_s, *, L):
    ka = pl.program_id(1)
    na = _hy_na(L)
    nk = na // 2 + 1

    @pl.when(ka == 0)
    def _():
        _hy_stage1(k_ref, s1_ref[...], ar_s, ai_s, na, nk, HIGHEST)

    xr, xi = _hy_cmul_tw(ar_s[ka], ai_s[ka], twr_ref[...], twi_ref[...], False)
    br, bi = _hy_dft256(fs_ref[...], xr, xi, False, HIGHEST)
    scale = 1.0 / (2.0 * L * norm_ref[...])
    kr_ref[...] = br * scale
    ki_ref[...] = bi * scale


def hyena_filter_spectrum(L, k_time, norm, tabs):
    na = _hy_na(L)
    nk = na // 2 + 1
    tc = HY_TC
    out = jax.ShapeDtypeStruct((nk, HY_R, 2 * HY_W), F32)
    return pl.pallas_call(
        functools.partial(_hy_spectrum_kernel, L=L),
        grid=(2 * HY_W // tc, nk),
        in_specs=[pl.BlockSpec((na, HY_R, tc), lambda j, ka: (0, 0, j)),
                  pl.BlockSpec((1, tc), lambda j, ka: (0, j)),
                  pl.BlockSpec((16 * nk, 8 * na), lambda j, ka: (0, 0)),
                  pl.BlockSpec((2 * HY_R, HY_R), lambda j, ka: (0, 0)),
                  pl.BlockSpec((None, HY_R, LANES), lambda j, ka: (ka, 0, 0)),
                  pl.BlockSpec((None, HY_R, LANES), lambda j, ka: (ka, 0, 0))],
        out_specs=(pl.BlockSpec((None, HY_R, tc), lambda j, ka: (ka, 0, j)),
                   pl.BlockSpec((None, HY_R, tc), lambda j, ka: (ka, 0, j))),
        out_shape=(out, out),
        scratch_shapes=[pltpu.VMEM((nk, HY_R, tc), F32), pltpu.VMEM((nk, HY_R, tc), F32)],
        compiler_params=_params("arbitrary", "arbitrary"),
        name="hyena_filter_spectrum",
    )(k_time.reshape(na, HY_R, 2 * HY_W), norm, tabs["s1_full"], tabs["fs"], tabs["tw_r"], tabs["tw_i"])


def _hy_conv_kernel(v_ref, x1_ref, x2_ref, cw_ref, cb_ref, skip_ref, s1_ref, fs_ref, g_ref, twr_ref, twi_ref,
                    kr_ref, ki_ref, o_ref, xin_s, ar_s, ai_s, *, L):
    s = pl.program_id(2)
    na = _hy_na(L)
    nh = na // 2
    nk = nh + 1
    tc = HY_TC
    ka = s % nk

    def short_conv(ref, part):
        row = lax.broadcasted_iota(jnp.int32, (L, 1), 0)
        u = ref[...].astype(F32)
        up = jnp.where(row == 0, 0.0, pltpu.roll(u, 1, 0))
        dn = jnp.where(row == L - 1, 0.0, pltpu.roll(u, L - 1, 0))
        w = cw_ref[part]
        y = w[0:1, :] * up + w[1:2, :] * u + w[2:3, :] * dn + cb_ref[part]
        return y.reshape(nh, HY_R, tc)

    @pl.when(s == 0)
    def _():
        xin_s[...] = short_conv(v_ref, 0)

    @pl.when(ka == 0)
    def _():
        _hy_stage1(xin_s, s1_ref[...].astype(BF16), ar_s, ai_s, nh, nk, None)

    fs = fs_ref[...].astype(BF16)
    twr, twi = twr_ref[...], twi_ref[...]
    xr, xi = _hy_cmul_tw(ar_s[ka], ai_s[ka], twr, twi, False)
    br, bi = _hy_dft256(fs, xr, xi, False, None)
    kr, ki = kr_ref[...], ki_ref[...]
    pr, pi_ = br * kr - bi * ki, br * ki + bi * kr
    qr, qi = _hy_dft256(fs, pr, pi_, True, None)
    qr, qi = _hy_cmul_tw(qr, qi, twr, twi, True)
    ar_s[ka] = qr
    ai_s[ka] = qi

    def finish(order):
        gm = g_ref[...].astype(BF16)
        sk = skip_ref[order]

        def body(rh, carry):
            rows = pl.ds(pl.multiple_of(rh * 8, 8), 8)
            q = jnp.concatenate([ar_s[:, rows, :].reshape(8 * nk, tc), ai_s[:, rows, :].reshape(8 * nk, tc)], axis=0)
            y = jnp.dot(gm, q.astype(BF16), preferred_element_type=F32).reshape(nh, 8, tc)
            xin_s[:, rows, :] = y + xin_s[:, rows, :] * sk
            return carry

        lax.fori_loop(0, HY_R // 8, body, 0)

    @pl.when(s == nk - 1)
    def _():
        finish(0)
        xin_s[...] = short_conv(x1_ref, 1) * xin_s[...]

    @pl.when(s == 2 * nk - 1)
    def _():
        finish(1)
        o_ref[...] = (short_conv(x2_ref, 2) * xin_s[...]).reshape(L, tc).astype(o_ref.dtype)


def hyena_conv(p_hy, row0, L, conv_w, conv_b, skip, kf_r, kf_i, tabs):
    na = _hy_na(L)
    nh, nk = na // 2, na // 2 + 1
    tc = HY_TC
    nct = HY_W // tc
    rb0 = row0 // L
    cw = conv_w.reshape(3, 3, HY_W).transpose(1, 0, 2)
    cb = conv_b.reshape(3, 1, HY_W)
    col = lambda part: pl.BlockSpec((L, tc), lambda b, c, s, part=part: (rb0 + b, part * nct + c))
    const2 = lambda shape: pl.BlockSpec(shape, lambda b, c, s: (0, 0))
    tw_spec = pl.BlockSpec((None, HY_R, LANES), lambda b, c, s: (s % nk, 0, 0))
    kf_spec = pl.BlockSpec((None, HY_R, tc), lambda b, c, s: (s % nk, 0, (s // nk) * nct + c))
    return pl.pallas_call(
        functools.partial(_hy_conv_kernel, L=L),
        grid=(B, nct, 2 * nk),
        in_specs=[col(0), col(1), col(2),
                  pl.BlockSpec((3, 3, tc), lambda b, c, s: (0, 0, c)),
                  pl.BlockSpec((3, 1, tc), lambda b, c, s: (0, 0, c)),
                  pl.BlockSpec((2, 1, tc), lambda b, c, s: (0, 0, c)),
                  const2((16 * nk, 8 * nh)), const2((2 * HY_R, HY_R)), const2((8 * nh, 16 * nk)),
                  tw_spec, tw_spec, kf_spec, kf_spec],
        out_specs=pl.BlockSpec((L, tc), lambda b, c, s: (b, c)),
        out_shape=jax.ShapeDtypeStruct((B * L, HY_W), BF16),
        scratch_shapes=[pltpu.VMEM((nh, HY_R, tc), F32), pltpu.VMEM((nk, HY_R, tc), F32),
                        pltpu.VMEM((nk, HY_R, tc), F32)],
        compiler_params=_params("arbitrary", "arbitrary", "arbitrary"),
        name="hyena_conv_%d" % L,
    )(p_hy, p_hy, p_hy, cw, cb, skip.reshape(2, 1, HY_W), tabs["s1_half"], tabs["fs"], tabs["g"],
      tabs["tw_r"], tabs["tw_i"], kf_r, kf_i)


def hyena_branch(p_hy, row0, L, conv_w, conv_b, w1, b1, w2, b2, w3, freq, skip):
    tabs = _hy_tables(L)
    k_time, norm = hyena_filter_time(L, w1, b1, w2, b2, w3, freq)
    kf_r, kf_i = hyena_filter_spectrum(L, k_time, norm, tabs)
    return hyena_conv(p_hy, row0, L, conv_w, conv_b, skip, kf_r, kf_i, tabs)


def route_jnp(aff, h2):
    def one(a, h, n, cap):
        a = a.reshape(B, n, N_EXPERTS)
        top_w, top_idx = lax.top_k(a.swapaxes(1, 2), cap)
        xs = jax.vmap(lambda hb, ib: hb[ib])(h.reshape(B, n, D), top_idx)
        return top_w, top_idx, xs.swapaxes(0, 1).reshape(N_EXPERTS, B * cap, D)

    wl, il, xl = one(aff[:NL, :N_EXPERTS], h2[:NL], SEQ, CAP_LAT)
    wc, ic, xc = one(aff[NL:, :N_EXPERTS], h2[NL:], CTX, CAP_CTX)
    xs = jnp.concatenate([xl, xc], axis=1)
    tw = jnp.concatenate([wl.swapaxes(0, 1).reshape(N_EXPERTS, -1), wc.swapaxes(0, 1).reshape(N_EXPERTS, -1)], axis=1)
    return xs, tw[..., None], il, ic


def combine_jnp(out, il, ic):
    def one(o, idx, n, cap):
        o = o.reshape(N_EXPERTS, B, cap, D).swapaxes(0, 1)
        f = lambda ib, ob: jnp.zeros((n, D), ob.dtype).at[ib.reshape(-1)].add(ob.reshape(-1, D))
        return jax.vmap(f)(idx, o).reshape(B * n, D)

    return jnp.concatenate([one(out[:, :B * CAP_LAT], il, SEQ, CAP_LAT), one(out[:, B * CAP_LAT:], ic, CTX, CAP_CTX)], axis=0)


def kernel(x, c, ctx, c_ctx, w_mod, b_mod, w_in, hy_conv_w, hy_conv_b, hy_w1, hy_b1, hy_w2, hy_b2, hy_w3, hy_freq, hy_skip, ssd_conv_w, ssd_conv_b, ssd_a_log, ssd_dt_bias, ssd_d, ssd_norm_g, da_lambda, da_subln_g, w_out, ln1_g, ln1_b, router, w_gate, w_up, w_down, ln2_g, ln2_b):
    xa = jnp.concatenate([x.reshape(NL, D), ctx.reshape(NC, D)], axis=0)
    cc = jnp.concatenate([c, c_ctx[None, :], jnp.zeros((8 - B - 1, D), F32)], axis=0)
    mod = modulation(cc, w_mod, b_mod).reshape(DEPTH, 8, 1, 6 * D)
    rope = rope_tables()
    o_ssd = HY_PROJ
    o_xbc = o_ssd + SSD_W
    o_dt = o_xbc + SSD_XBC
    o_da = o_dt + 2 * SSD_H
    w_hy = w_in[:, :, :o_ssd].astype(BF16)
    w_ssd = jnp.concatenate([w_in[:, :, o_xbc:o_dt], w_in[:, :, o_ssd:o_xbc], w_in[:, :, o_dt:o_da],
                             jnp.zeros((DEPTH, D, LANES - 2 * SSD_H), F32)], axis=2).astype(BF16)
    w_da = w_in[:, :, o_da:].astype(BF16)
    w_out_b = w_out.astype(BF16)
    r_pad = jnp.pad(router, ((0, 0), (0, 0), (0, LANES - N_EXPERTS))).astype(BF16)
    for l in range(DEPTH):
        lam_init = 0.8 - 0.6 * math.exp(-0.3 * l)
        p_hy = in_projection(xa, mod[l], w_hy, l, "hy")
        zx, dt = in_projection(xa, mod[l], w_ssd, l, "ssd")
        da = in_projection(xa, mod[l], w_da, l, "da", rope)

        hy_args = (hy_conv_w[l], hy_conv_b[l], hy_w1[l], hy_b1[l], hy_w2[l], hy_b2[l], hy_w3[l], hy_freq[l], hy_skip[l])
        y_hy = jnp.concatenate([hyena_branch(p_hy, 0, SEQ, *hy_args), hyena_branch(p_hy, NL, CTX, *hy_args)], axis=0)

        yf, yb = ssd_scan(zx, dt, ssd_conv_w[l], ssd_conv_b[l], ssd_a_log[l], ssd_dt_bias[l], ssd_d[l])
        y_ssd = ssd_gate(yf, yb, zx, ssd_norm_g[l])

        lp = da_lambda[l].astype(F32)
        lam = jnp.exp(jnp.sum(lp[0] * lp[1])) - jnp.exp(jnp.sum(lp[2] * lp[3])) + lam_init
        y_da = jnp.concatenate([diff_attention(da, lam, da_subln_g[l], lam_init, False),
                                diff_attention(da, lam, da_subln_g[l], lam_init, True)], axis=0)

        x1, h2, aff = out_projection(y_hy, y_ssd, y_da, w_out_b, l, xa, mod[l], ln1_g[l], ln1_b[l], r_pad)
        xs, tw, il, ic = route_jnp(aff, h2)
        eo = expert_ffn(xs, tw, w_gate, w_up, w_down, l)
        moe = combine_jnp(eo, il, ic)
        xa = moe_residual(x1, moe, mod[l], ln2_g[l], ln2_b[l])
    return xa[:NL].reshape(B, SEQ, D)
```

```python
import functools
import math

import numpy as np
import jax
import jax.numpy as jnp
from jax import lax
from jax.experimental import pallas as pl
from jax.experimental.pallas import tpu as pltpu

F32 = jnp.float32
BF16 = jnp.bfloat16
HIGHEST = lax.Precision.HIGHEST

D = 2048
B = 4
SEQ = 4096
DEPTH = 4
GRID_W = 64
CTX = 256
HY_W = 512
SSD_W = 512
DA_W = 1024
HY_EMB = 33
HY_BANDS = 16
HY_HID = 64
HY_MIN_DECAY = math.log(1e-2) / 1.5
HY_MAX_DECAY = math.log(1e-2) / 0.3
HY_PROJ = 3 * HY_W
SSD_H = 8
SSD_N = 128
SSD_CHUNK = 128
SSD_XBC = 1024
DA_H = 8
DA_DK = 64
DA_QK = 1024
ROPE_THETA = 10000.0
ROPE_PAIRS = 16
N_EXPERTS = 16
EC_FACTOR = 2
D_FF = 1536
ALPHA = (2 * DEPTH) ** 0.25
LN_EPS = 1e-6

NL = B * SEQ
NC = B * CTX
M = NL + NC
LANES = 128
TM = 256
VMEM_LIMIT = 56 * 1024 * 1024

N_LAT_TILES = NL // TM
TILES_PER_SAMPLE = SEQ // TM


def _mod_row(i):
    return jnp.where(i < N_LAT_TILES, i // TILES_PER_SAMPLE, B)


def _mod_spec(chunk):
    return pl.BlockSpec((None, 1, D), lambda i, chunk=chunk: (_mod_row(i), 0, chunk))


def _params(*sem):
    return pltpu.CompilerParams(dimension_semantics=sem, vmem_limit_bytes=VMEM_LIMIT)


def _ln(x):
    mu = jnp.mean(x, axis=-1, keepdims=True)
    xc = x - mu
    var = jnp.mean(xc * xc, axis=-1, keepdims=True)
    return xc * lax.rsqrt(var + LN_EPS)


def _silu(x):
    return x * jax.nn.sigmoid(x)


def _mod_kernel(c_ref, w_ref, b_ref, o_ref):
    s = _silu(c_ref[...])
    o_ref[...] = jnp.dot(s, w_ref[...], precision=HIGHEST, preferred_element_type=F32) + b_ref[...]


def modulation(cc, w_mod, b_mod):
    tn = 1024
    return pl.pallas_call(
        _mod_kernel,
        grid=(DEPTH, 6 * D // tn),
        in_specs=[
            pl.BlockSpec((8, D), lambda l, j: (0, 0)),
            pl.BlockSpec((None, D, tn), lambda l, j: (l, 0, j)),
            pl.BlockSpec((None, 1, tn), lambda l, j: (l, 0, j)),
        ],
        out_specs=pl.BlockSpec((None, 8, tn), lambda l, j: (l, 0, j)),
        out_shape=jax.ShapeDtypeStruct((DEPTH, 8, 6 * D), F32),
        compiler_params=_params("arbitrary", "arbitrary"),
        name="modulation",
    )(cc, w_mod, b_mod.reshape(DEPTH, 1, 6 * D))


def _inproj_kernel(x_ref, sh_ref, sc_ref, w_ref, *rest, mode):
    h = _ln(x_ref[...]) * (1.0 + sc_ref[...]) + sh_ref[...]
    acc = jnp.dot(h.astype(BF16), w_ref[...], preferred_element_type=F32)
    if mode == "hy":
        (o_ref,) = rest
        o_ref[...] = acc.astype(o_ref.dtype)
    elif mode == "ssd":
        o_ref, dt_ref = rest
        o_ref[...] = acc[:, : SSD_XBC + SSD_W].astype(o_ref.dtype)
        dt_ref[...] = acc[:, SSD_XBC + SSD_W:]
    else:
        cos_ref, sin_ref, o_ref = rest
        cs = cos_ref[...]
        sn = sin_ref[...]
        lane = lax.broadcasted_iota(jnp.int32, (TM, LANES), 1)
        is_a = (lane % 32) < ROPE_PAIRS
        for g in range(2 * DA_QK // LANES):
            blk = acc[:, g * LANES:(g + 1) * LANES]
            swapped = jnp.where(is_a, pltpu.roll(blk, LANES - ROPE_PAIRS, 1), pltpu.roll(blk, ROPE_PAIRS, 1))
            rot = blk * cs + swapped * sn
            if g < DA_QK // LANES:
                rot = rot * (DA_DK ** -0.5 * math.log2(math.e))
            o_ref[:, g * LANES:(g + 1) * LANES] = rot.astype(o_ref.dtype)
        o_ref[:, 2 * DA_QK:] = acc[:, 2 * DA_QK:].astype(o_ref.dtype)


def in_projection(x, mod_l, w, l, mode, rope=None):
    n = w.shape[2]
    in_specs = [
        pl.BlockSpec((TM, D), lambda i: (i, 0)),
        _mod_spec(0),
        _mod_spec(1),
        pl.BlockSpec((None, D, n), lambda i: (l, 0, 0)),
    ]
    args = [x, mod_l, mod_l, w]
    if mode == "hy":
        out_shape = jax.ShapeDtypeStruct((M, n), BF16)
        out_specs = pl.BlockSpec((TM, n), lambda i: (i, 0))
    elif mode == "ssd":
        n0 = SSD_XBC + SSD_W
        out_shape = (jax.ShapeDtypeStruct((M, n0), BF16), jax.ShapeDtypeStruct((M, LANES), F32))
        out_specs = (pl.BlockSpec((TM, n0), lambda i: (i, 0)), pl.BlockSpec((TM, LANES), lambda i: (i, 0)))
    else:
        rope_idx = lambda i: (jnp.where(i < N_LAT_TILES, i % TILES_PER_SAMPLE, TILES_PER_SAMPLE), 0)
        in_specs += [pl.BlockSpec((TM, LANES), rope_idx), pl.BlockSpec((TM, LANES), rope_idx)]
        args += list(rope)
        out_shape = jax.ShapeDtypeStruct((M, n), BF16)
        out_specs = pl.BlockSpec((TM, n), lambda i: (i, 0))
    return pl.pallas_call(
        functools.partial(_inproj_kernel, mode=mode),
        grid=(M // TM,),
        in_specs=in_specs,
        out_specs=out_specs,
        out_shape=out_shape,
        compiler_params=_params("arbitrary"),
        name="inproj_" + mode,
    )(*args)


def rope_tables():
    pos = np.arange(SEQ)
    r = (pos // GRID_W).astype(np.float64)
    col = (pos % GRID_W).astype(np.float64)
    inv = ROPE_THETA ** (-np.arange(ROPE_PAIRS, dtype=np.float64) / ROPE_PAIRS)
    inv = inv.astype(np.float32).astype(np.float64)
    lane = np.arange(LANES)
    d = lane % DA_DK
    axis = d // 32
    pair = d % ROPE_PAIRS
    is_b = (d % 32) >= ROPE_PAIRS
    ang = np.where(axis[None, :] == 0, r[:, None], col[:, None]) * inv[pair][None, :]
    cos = np.cos(ang)
    sin = np.where(is_b[None, :], np.sin(ang), -np.sin(ang))
    cos = np.concatenate([cos, np.ones((TM, LANES))], axis=0)
    sin = np.concatenate([sin, np.zeros((TM, LANES))], axis=0)
    return jnp.asarray(cos, F32), jnp.asarray(sin, F32)


def _attn_kernel(lam_ref, q_ref, *rest, n_seg, post_scale):
    kv = rest[: 2 * n_seg]
    g_ref, o_ref = rest[2 * n_seg:]
    lam = lam_ref[0]
    q = q_ref[...]
    lane = lax.broadcasted_iota(jnp.int32, q.shape, 1)
    zero = jnp.zeros_like(q)
    qm = (jnp.where(lane < DA_DK, q, zero), jnp.where(lane >= DA_DK, q, zero))
    nt = (((1,), (1,)), ((), ()))
    scores = [[lax.dot_general(qm[m], kv[2 * s][...], nt, preferred_element_type=F32) for s in range(n_seg)]
              for m in range(2)]
    probs = []
    for m in range(2):
        mx = functools.reduce(jnp.maximum, [jnp.max(s, axis=-1, keepdims=True) for s in scores[m]])
        es = [jnp.exp2(s - mx) for s in scores[m]]
        tot = functools.reduce(lambda a, b: a + b, [jnp.sum(e, axis=-1, keepdims=True) for e in es])
        probs.append((es, 1.0 / tot))
    out = None
    for s in range(n_seg):
        a = probs[0][0][s] * probs[0][1] - (lam * probs[1][1]) * probs[1][0][s]
        o = jnp.dot(a.astype(BF16), kv[2 * s + 1][...], preferred_element_type=F32)
        out = o if out is None else out + o
    out = out * lax.rsqrt(jnp.mean(out * out, axis=-1, keepdims=True) + LN_EPS)
    o_ref[...] = ((out * g_ref[...]) * post_scale).astype(o_ref.dtype)


def diff_attention(da, lam, subln_g, lam_init, ctx_queries):
    tq = 256
    lam = jnp.reshape(lam, (1,)).astype(F32)
    g = subln_g.reshape(1, LANES)
    kc_spec = pl.BlockSpec((CTX, LANES), lambda b, h, i: (NL // CTX + b, DA_H + h))
    vc_spec = pl.BlockSpec((CTX, LANES), lambda b, h, i: (NL // CTX + b, 2 * DA_H + h))
    if ctx_queries:
        rows, nq = NC, CTX // tq
        q_spec = pl.BlockSpec((tq, LANES), lambda b, h, i: ((NL + b * CTX) // tq + i, h))
        segs, seg_args = [kc_spec, vc_spec], [da, da]
    else:
        rows, nq = NL, SEQ // tq
        q_spec = pl.BlockSpec((tq, LANES), lambda b, h, i: (b * nq + i, h))
        kl_spec = pl.BlockSpec((SEQ, LANES), lambda b, h, i: (b, DA_H + h))
        vl_spec = pl.BlockSpec((SEQ, LANES), lambda b, h, i: (b, 2 * DA_H + h))
        segs, seg_args = [kl_spec, vl_spec, kc_spec, vc_spec], [da, da, da, da]
    return pl.pallas_call(
        functools.partial(_attn_kernel, n_seg=len(segs) // 2, post_scale=1.0 - lam_init),
        grid=(B, DA_H, nq),
        in_specs=[pl.BlockSpec(memory_space=pltpu.SMEM), q_spec] + segs + [pl.BlockSpec((1, LANES), lambda b, h, i: (0, 0))],
        out_specs=pl.BlockSpec((tq, LANES), lambda b, h, i: (b * nq + i, h)),
        out_shape=jax.ShapeDtypeStruct((rows, DA_W), BF16),
        compiler_params=_params("arbitrary", "arbitrary", "arbitrary"),
        name="diff_attn_ctx" if ctx_queries else "diff_attn_lat",
    )(lam, da, *seg_args, g)


SSD_NC_CTX = CTX // SSD_CHUNK
SSD_NC_LAT = SEQ // SSD_CHUNK
SSD_STEPS = SSD_NC_CTX + SSD_NC_LAT


def _ssd_chunk(s, direction):
    in_ctx = s < SSD_NC_CTX
    k = jnp.where(in_ctx, s, s - SSD_NC_CTX)
    nck = jnp.where(in_ctx, SSD_NC_CTX, SSD_NC_LAT)
    if direction == 1:
        k = nck - 1 - k
    return in_ctx, k, nck


def _ssd_rowblock(b, s, direction):
    in_ctx, k, _ = _ssd_chunk(s, direction)
    return jnp.where(in_ctx, NL // SSD_CHUNK + b * SSD_NC_CTX + k, b * SSD_NC_LAT + k)


def _ssd_kernel(xf_ref, xfp_ref, xfn_ref, dtf_ref, xb_ref, xbp_ref, xbn_ref, dtb_ref,
                cw_ref, cb_ref, a_ref, dtbias_ref, dskip_ref, yf_ref, yb_ref, state_ref):
    s = pl.program_id(1)
    q = SSD_CHUNK

    @pl.when(s == 0)
    def _():
        state_ref[...] = jnp.zeros_like(state_ref)

    row = lax.broadcasted_iota(jnp.int32, (q, 1), 0)
    ri = lax.broadcasted_iota(jnp.int32, (q, q), 0)
    ci = lax.broadcasted_iota(jnp.int32, (q, q), 1)
    lane_lo = lax.broadcasted_iota(jnp.int32, (1, LANES), 1) < 64
    cw = cw_ref[...]
    dirs = ((xf_ref, xfp_ref, xfn_ref, dtf_ref, yf_ref), (xb_ref, xbp_ref, xbn_ref, dtb_ref, yb_ref))
    for d, (x_ref, xp_ref, xn_ref, dt_ref, y_ref) in enumerate(dirs):
        _, k, nck = _ssd_chunk(s, d)
        u = x_ref[...].astype(F32)
        prev = jnp.where(k == 0, 0.0, xp_ref[15:16, :].astype(F32))
        nxt = jnp.where(k == nck - 1, 0.0, xn_ref[0:1, :].astype(F32))
        up = jnp.where(row == 0, prev, pltpu.roll(u, 1, 0))
        dn = jnp.where(row == q - 1, nxt, pltpu.roll(u, q - 1, 0))
        conv = cw[0:1, :] * up + cw[1:2, :] * u + cw[2:3, :] * dn + cb_ref[...]
        xbc = _silu(conv)
        xs = xbc[:, :SSD_W]
        bm = xbc[:, SSD_W:SSD_W + 2 * SSD_N]
        cm = xbc[:, SSD_W + 2 * SSD_N:]
        z = dt_ref[...] + dtbias_ref[...]
        dt = jnp.maximum(z, 0.0) + jnp.log(1.0 + jnp.exp(-jnp.abs(z)))
        a = dt * a_ref[...]
        tri = (ci <= ri) if d == 0 else (ci >= ri)
        a_cum = jnp.dot(tri.astype(F32), a, precision=HIGHEST, preferred_element_type=F32)
        a_cum_t = a_cum.T
        ea = jnp.exp(a_cum)
        a_tot = a_cum[q - 1:q, :] if d == 0 else a_cum[0:1, :]
        to_end = jnp.exp(a_tot - a_cum)
        chunk_decay = jnp.exp(a_tot)

        def halves(col_arr, c0, c1):
            return jnp.where(lane_lo, col_arr[:, c0:c0 + 1], col_arr[:, c1:c1 + 1])

        for g in range(2):
            bg = bm[:, g * SSD_N:(g + 1) * SSD_N]
            cgb = cm[:, g * SSD_N:(g + 1) * SSD_N].astype(BF16)
            gmat = lax.dot_general(cgb, bg.astype(BF16), (((1,), (1,)), ((), ())), preferred_element_type=F32)
            bgt = bg.T.astype(BF16)
            for pr in range(2):
                p = 2 * g + pr
                c0, c1 = 8 * d + 2 * p, 8 * d + 2 * p + 1
                xpair = xs[:, p * LANES:(p + 1) * LANES]
                xdt = xpair * halves(dt, c0, c1)
                y = None
                for hh, hc in ((0, c0), (1, c1)):
                    seg = a_cum[:, hc:hc + 1] - a_cum_t[hc:hc + 1, :]
                    lmat = jnp.exp(jnp.where(tri, seg, -jnp.inf))
                    xm = jnp.where(lane_lo if hh == 0 else jnp.logical_not(lane_lo), xdt, 0.0)
                    t = jnp.dot((gmat * lmat).astype(BF16), xm.astype(BF16), preferred_element_type=F32)
                    y = t if y is None else y + t
                st = state_ref[d, p]
                y = y + halves(ea, c0, c1) * jnp.dot(cgb, st.astype(BF16), preferred_element_type=F32)
                upd = jnp.dot(bgt, (xdt * halves(to_end, c0, c1)).astype(BF16), preferred_element_type=F32)
                state_ref[d, p] = halves(chunk_decay, c0, c1) * st + upd
                if d == 0:
                    y = y + dskip_ref[:, p * LANES:(p + 1) * LANES] * xpair
                y_ref[:, p * LANES:(p + 1) * LANES] = y


def ssd_scan(zx, dt, conv_w, conv_b, a_log, dt_bias, d_skip):
    n16 = M // 16

    def chunk_spec(direction):
        return pl.BlockSpec((SSD_CHUNK, SSD_XBC), lambda b, s: (_ssd_rowblock(b, s, direction), 0))

    def prev_spec(direction):
        return pl.BlockSpec((16, SSD_XBC), lambda b, s: (jnp.maximum(_ssd_rowblock(b, s, direction) * 8 - 1, 0), 0))

    def next_spec(direction):
        return pl.BlockSpec((16, SSD_XBC), lambda b, s: (jnp.minimum(_ssd_rowblock(b, s, direction) * 8 + 8, n16 - 1), 0))

    def dt_spec(direction):
        return pl.BlockSpec((SSD_CHUNK, LANES), lambda b, s: (_ssd_rowblock(b, s, direction), 0))

    def y_spec(direction):
        return pl.BlockSpec((SSD_CHUNK, SSD_W), lambda b, s: (_ssd_rowblock(b, s, direction), 0))

    const = lambda shape: pl.BlockSpec(shape, lambda b, s: (0, 0))
    pad16 = lambda v: jnp.pad(v.reshape(1, 2 * SSD_H).astype(F32), ((0, 0), (0, LANES - 2 * SSD_H)))
    a_row = pad16(-jnp.exp(a_log.astype(F32)))
    bias_row = pad16(dt_bias)
    dskip_row = jnp.repeat(d_skip.astype(F32), SSD_W // SSD_H).reshape(1, SSD_W)
    return pl.pallas_call(
        _ssd_kernel,
        grid=(B, SSD_STEPS),
        in_specs=[chunk_spec(0), prev_spec(0), next_spec(0), dt_spec(0),
                  chunk_spec(1), prev_spec(1), next_spec(1), dt_spec(1),
                  const((3, SSD_XBC)), const((1, SSD_XBC)), const((1, LANES)), const((1, LANES)), const((1, SSD_W))],
        out_specs=(y_spec(0), y_spec(1)),
        out_shape=(jax.ShapeDtypeStruct((M, SSD_W), F32), jax.ShapeDtypeStruct((M, SSD_W), F32)),
        scratch_shapes=[pltpu.VMEM((2, 4, SSD_N, LANES), F32)],
        compiler_params=_params("arbitrary", "arbitrary"),
        name="ssd_scan",
    )(zx, zx, zx, dt, zx, zx, zx, dt, conv_w, conv_b.reshape(1, SSD_XBC), a_row, bias_row, dskip_row)


def _ssd_gate_kernel(yf_ref, yb_ref, z_ref, g_ref, o_ref):
    y = (yf_ref[...] + yb_ref[...]) * _silu(z_ref[...].astype(F32))
    y = y * lax.rsqrt(jnp.mean(y * y, axis=-1, keepdims=True) + LN_EPS)
    o_ref[...] = (y * g_ref[...]).astype(o_ref.dtype)


def ssd_gate(yf, yb, zx, norm_g):
    tm = 512
    row = lambda i: (i, 0)
    return pl.pallas_call(
        _ssd_gate_kernel,
        grid=(M // tm,),
        in_specs=[pl.BlockSpec((tm, SSD_W), row), pl.BlockSpec((tm, SSD_W), row),
                  pl.BlockSpec((tm, SSD_W), lambda i: (i, SSD_XBC // SSD_W)),
                  pl.BlockSpec((1, SSD_W), lambda i: (0, 0))],
        out_specs=pl.BlockSpec((tm, SSD_W), row),
        out_shape=jax.ShapeDtypeStruct((M, SSD_W), BF16),
        compiler_params=_params("arbitrary"),
        name="ssd_gate",
    )(yf, yb, zx, norm_g.reshape(1, SSD_W))


def _pack_halves(v):
    n = v.shape[-1] // 2
    lo = lax.shift_right_logical(lax.bitcast_convert_type(v[:, :n], jnp.int32), 16)
    hi = lax.bitcast_convert_type(v[:, n:], jnp.int32) & jnp.int32(-65536)
    return hi | lo


def _unpack_halves(w):
    lo = lax.bitcast_convert_type(lax.shift_left(w, 16), F32).astype(BF16)
    hi = lax.bitcast_convert_type(w & jnp.int32(-65536), F32).astype(BF16)
    return lo, hi


def _outproj_kernel(yh_ref, ys_ref, ya_ref, w_ref, x_ref, g1_ref, lng_ref, lnb_ref, sh2_ref, sc2_ref, r_ref,
                    x1_ref, h2_ref, aff_ref):
    y = jnp.dot(yh_ref[...], w_ref[0:HY_W, :], preferred_element_type=F32)
    y = y + jnp.dot(ys_ref[...], w_ref[HY_W:HY_W + SSD_W, :], preferred_element_type=F32)
    y = y + jnp.dot(ya_ref[...], w_ref[HY_W + SSD_W:, :], preferred_element_type=F32)
    x1 = _ln(ALPHA * x_ref[...] + g1_ref[...] * y) * lng_ref[...] + lnb_ref[...]
    x1_ref[...] = x1
    h2 = (_ln(x1) * (1.0 + sc2_ref[...]) + sh2_ref[...]).astype(BF16)
    h2_ref[...] = _pack_halves(h2.astype(F32))
    logits = jnp.dot(h2, r_ref[...], preferred_element_type=F32)
    lane = lax.broadcasted_iota(jnp.int32, logits.shape, 1)
    logits = jnp.where(lane < N_EXPERTS, logits, -jnp.inf)
    e = jnp.exp(logits - jnp.max(logits, axis=-1, keepdims=True))
    aff_ref[...] = e / jnp.sum(e, axis=-1, keepdims=True)


def out_projection(yh, ys, ya, w_out, l, x, mod_l, ln_g, ln_b, r_pad):
    row = lambda i: (i, 0)
    const = lambda i: (0, 0)
    layer = lambda i: (l, 0, 0)
    return pl.pallas_call(
        _outproj_kernel,
        grid=(M // TM,),
        in_specs=[pl.BlockSpec((TM, HY_W), row), pl.BlockSpec((TM, SSD_W), row), pl.BlockSpec((TM, DA_W), row),
                  pl.BlockSpec((None, D, D), layer), pl.BlockSpec((TM, D), row), _mod_spec(2),
                  pl.BlockSpec((1, D), const), pl.BlockSpec((1, D), const), _mod_spec(3), _mod_spec(4),
                  pl.BlockSpec((None, D, LANES), layer)],
        out_specs=(pl.BlockSpec((TM, D), row), pl.BlockSpec((TM, D // 2), row), pl.BlockSpec((TM, LANES), row)),
        out_shape=(jax.ShapeDtypeStruct((M, D), F32), jax.ShapeDtypeStruct((M, D // 2), jnp.int32),
                   jax.ShapeDtypeStruct((M, LANES), F32)),
        compiler_params=_params("arbitrary"),
        name="outproj",
    )(yh, ys, ya, w_out, x, mod_l, ln_g.reshape(1, D), ln_b.reshape(1, D), mod_l, mod_l, r_pad)


CAP_LAT = EC_FACTOR * SEQ // N_EXPERTS
CAP_CTX = EC_FACTOR * CTX // N_EXPERTS
EROWS = B * (CAP_LAT + CAP_CTX)
FFN_TR = EROWS // 2
FFN_TF = 256


def _gather_copy(h2_hbm, raw_s, sem, src_row, dst_row):
    return pltpu.make_async_copy(h2_hbm.at[pl.ds(src_row, 1), :], raw_s.at[pl.ds(dst_row, 1), :], sem)


def _ffn_kernel(idx_ref, idxn_ref, h2_hbm, tw_ref, wg_ref, wu_ref, wd_ref, o_ref, raw_s, xlo_s, xhi_s, acc_s, sem):
    e, r, f = pl.program_id(0), pl.program_id(1), pl.program_id(2)
    half = D // 2

    def start_gather(iref):
        def body(i, carry):
            _gather_copy(h2_hbm, raw_s, sem, iref[0, i], i).start()
            return carry
        lax.fori_loop(0, EROWS, body, 0)

    def wait_gather():
        def body(i, carry):
            _gather_copy(h2_hbm, raw_s, sem, 0, i).wait()
            return carry
        lax.fori_loop(0, EROWS, body, 0)

    @pl.when((r == 0) & (f == 0))
    def _():
        @pl.when(e == 0)
        def _():
            start_gather(idx_ref)

        wait_gather()
        lo, hi = _unpack_halves(raw_s[...])
        xlo_s[...] = lo
        xhi_s[...] = hi

        @pl.when(e + 1 < N_EXPERTS)
        def _():
            start_gather(idxn_ref)

    rows = pl.ds(pl.multiple_of(r * FFN_TR, 16), FFN_TR)
    xlo = xlo_s[rows, :]
    xhi = xhi_s[rows, :]

    def proj(w_ref):
        w = w_ref[...].astype(BF16)
        return (jnp.dot(xlo, w[:half], preferred_element_type=F32) + jnp.dot(xhi, w[half:], preferred_element_type=F32))

    g = proj(wg_ref)
    u = proj(wu_ref)
    t = jnp.dot((_silu(g) * u).astype(BF16), wd_ref[...].astype(BF16), preferred_element_type=F32)

    @pl.when(f == 0)
    def _():
        acc_s[...] = t

    @pl.when(f != 0)
    def _():
        acc_s[...] += t

    @pl.when(f == D_FF // FFN_TF - 1)
    def _():
        y = (acc_s[...] * tw_ref[...]).astype(BF16).astype(F32)
        o_ref[...] = _pack_halves(y)


def expert_ffn(idx, h2p, tw, w_gate, w_up, w_down, l):
    smem_idx = lambda shift: pl.BlockSpec((None, 1, EROWS), lambda e, r, f: (jnp.minimum(e + shift, N_EXPERTS - 1), 0, 0),
                                          memory_space=pltpu.SMEM)
    return pl.pallas_call(
        _ffn_kernel,
        grid=(N_EXPERTS, EROWS // FFN_TR, D_FF // FFN_TF),
        in_specs=[smem_idx(0), smem_idx(1),
                  pl.BlockSpec(memory_space=pl.ANY),
                  pl.BlockSpec((None, FFN_TR, 1), lambda e, r, f: (e, r, 0)),
                  pl.BlockSpec((None, None, D, FFN_TF), lambda e, r, f: (l, e, 0, f)),
                  pl.BlockSpec((None, None, D, FFN_TF), lambda e, r, f: (l, e, 0, f)),
                  pl.BlockSpec((None, None, FFN_TF, D), lambda e, r, f: (l, e, f, 0))],
        out_specs=pl.BlockSpec((None, FFN_TR, D // 2), lambda e, r, f: (e, r, 0)),
        out_shape=jax.ShapeDtypeStruct((N_EXPERTS, EROWS, D // 2), jnp.int32),
        scratch_shapes=[pltpu.VMEM((EROWS, D // 2), jnp.int32), pltpu.VMEM((EROWS, D // 2), BF16),
                        pltpu.VMEM((EROWS, D // 2), BF16), pltpu.VMEM((FFN_TR, D), F32),
                        pltpu.SemaphoreType.DMA(())],
        compiler_params=_params("arbitrary", "arbitrary", "arbitrary"),
        name="expert_ffn",
    )(idx, idx, h2p, tw, w_gate, w_up, w_down)


SEL_BLK = 256


def _select_kernel(aff_ref, pos_ref, cex_ref, cin_ref, idx_ref, tw_ref, *, n, cap, row_base, slot_base):
    b = pl.program_id(0)
    aff = aff_ref[...]
    bits = lax.bitcast_convert_type(aff, jnp.int32)

    def search(i, thr):
        cand = thr | jnp.left_shift(jnp.int32(1), 30 - i)
        cnt = jnp.sum((bits >= cand).astype(jnp.int32), axis=0, keepdims=True)
        return jnp.where(cnt >= cap, cand, thr)

    thr = lax.fori_loop(0, 31, search, jnp.zeros((1, LANES), jnp.int32))
    gt = bits > thr
    eq = bits == thr
    need = (cap - jnp.sum(gt.astype(jnp.int32), axis=0, keepdims=True)).astype(F32)
    ri = lax.broadcasted_iota(jnp.int32, (SEL_BLK, SEL_BLK), 0)
    ci = lax.broadcasted_iota(jnp.int32, (SEL_BLK, SEL_BLK), 1)
    tri = (ci <= ri).astype(BF16)

    def prefix(mask):
        m = mask.astype(BF16)
        off = jnp.zeros((1, LANES), F32)
        parts = []
        for k in range(n // SEL_BLK):
            p = jnp.dot(tri, m[k * SEL_BLK:(k + 1) * SEL_BLK], preferred_element_type=F32) + off
            parts.append(p)
            off = p[SEL_BLK - 1:SEL_BLK, :]
        return jnp.concatenate(parts, axis=0) if len(parts) > 1 else parts[0]

    sel = gt | (eq & (prefix(eq) <= need))
    cnt = prefix(sel)
    base = slot_base + b * cap
    key = jnp.where(sel, cnt - 1.0, -1.0)
    pos_ref[...] = jnp.where(sel, cnt.astype(jnp.int32) - 1 + base, -1)
    cin_ref[...] = cnt.astype(jnp.int32) + base
    cex_ref[...] = cnt.astype(jnp.int32) - sel.astype(jnp.int32) + base
    slots = lax.broadcasted_iota(jnp.int32, (1, cap), 1).astype(F32)
    tok = (lax.broadcasted_iota(jnp.int32, (n, 1), 0) + (row_base + b * n)).astype(F32)
    for e in range(N_EXPERTS):
        hit = key[:, e:e + 1] == slots
        idx_ref[e:e + 1, :] = jnp.sum(jnp.where(hit, tok, 0.0), axis=0, keepdims=True).astype(jnp.int32)
        tw_ref[e:e + 1, :] = jnp.sum(jnp.where(hit, aff[:, e:e + 1], 0.0), axis=0, keepdims=True)


def moe_select(aff, n, cap, row_base, slot_base):
    rb = row_base // n
    tok = jax.ShapeDtypeStruct((B * n, LANES), jnp.int32)
    tok_spec = pl.BlockSpec((n, LANES), lambda b: (b, 0))
    return pl.pallas_call(
        functools.partial(_select_kernel, n=n, cap=cap, row_base=row_base, slot_base=slot_base),
        grid=(B,),
        in_specs=[pl.BlockSpec((n, LANES), lambda b: (rb + b, 0))],
        out_specs=(tok_spec, tok_spec, tok_spec,
                   pl.BlockSpec((None, N_EXPERTS, cap), lambda b: (b, 0, 0)),
                   pl.BlockSpec((None, N_EXPERTS, cap), lambda b: (b, 0, 0))),
        out_shape=(tok, tok, tok, jax.ShapeDtypeStruct((B, N_EXPERTS, cap), jnp.int32),
                   jax.ShapeDtypeStruct((B, N_EXPERTS, cap), F32)),
        compiler_params=_params("arbitrary"),
        name="moe_select_%d" % n,
    )(aff)


CMB_T = 128
CMB_J = 40
CMB_STEP = CMB_J - 8
CMB_K = N_EXPERTS * CMB_J
CMB_TILES = M // CMB_T


def _window_start(lo):
    return jnp.minimum((lo // 8) * 8, EROWS - CMB_J)


def _combine_kernel(starts_ref, nr_ref, pos_ref, y_hbm, ex_ref, x1_ref, g2_ref, lng_ref, lnb_ref, o_ref, wbuf, sem):
    i = pl.program_id(0)
    slot = i % 2

    def copies(tile, rnd, slot_):
        out = []
        for e in range(N_EXPERTS):
            st = _window_start(starts_ref[tile * N_EXPERTS + e] + rnd * CMB_STEP)
            out.append(pltpu.make_async_copy(y_hbm.at[e, pl.ds(pl.multiple_of(st, 8), CMB_J), :],
                                             wbuf.at[slot_, pl.ds(e * CMB_J, CMB_J), :], sem.at[slot_]))
        return out

    @pl.when(i == 0)
    def _():
        for cp in copies(0, 0, 0):
            cp.start()

    @pl.when(i + 1 < CMB_TILES)
    def _():
        for cp in copies(i + 1, 0, 1 - slot):
            cp.start()

    lane = lax.broadcasted_iota(jnp.int32, (1, LANES), 1)
    jmod = (lax.broadcasted_iota(jnp.int32, (1, CMB_K), 1) % CMB_J).astype(F32)
    pos = pos_ref[...]

    def scatter_round(rnd):
        lov = jnp.zeros((1, LANES), jnp.int32)
        for e in range(N_EXPERTS):
            lov = jnp.where(lane == e, starts_ref[i * N_EXPERTS + e] + rnd * CMB_STEP, lov)
        stv = _window_start(lov)
        valid = (pos >= lov) & (pos < lov + CMB_STEP) & (lane < N_EXPERTS)
        rel = jnp.where(valid, pos - stv, -1).astype(F32).astype(BF16)
        spread = jnp.dot(rel, ex_ref[...], preferred_element_type=F32)
        onehot = (spread == jmod).astype(BF16)
        lo, hi = _unpack_halves(wbuf[slot])
        return jnp.concatenate([jnp.dot(onehot, lo, preferred_element_type=F32),
                                jnp.dot(onehot, hi, preferred_element_type=F32)], axis=1)

    for cp in copies(i, 0, slot):
        cp.wait()
    moe = scatter_round(0)

    def extra(rnd, acc):
        for cp in copies(i, rnd, slot):
            cp.start()
        for cp in copies(i, rnd, slot):
            cp.wait()
        return acc + scatter_round(rnd)

    moe = lax.fori_loop(1, nr_ref[i], extra, moe)
    o_ref[...] = _ln(ALPHA * x1_ref[...] + g2_ref[...] * moe) * lng_ref[...] + lnb_ref[...]


def moe_combine(starts, rounds, pos, y, x1, mod_l, ln_g, ln_b):
    ex = np.zeros((LANES, CMB_K), np.float32)
    for e in range(N_EXPERTS):
        ex[e, e * CMB_J:(e + 1) * CMB_J] = 1.0
    lat_tiles = NL // CMB_T
    mod_row = lambda i: jnp.where(i < lat_tiles, i // (SEQ // CMB_T), B)
    row = lambda i, *_: (i, 0)
    const = lambda i, *_: (0, 0)
    grid_spec = pltpu.PrefetchScalarGridSpec(
        num_scalar_prefetch=2,
        grid=(CMB_TILES,),
        in_specs=[pl.BlockSpec((CMB_T, LANES), row), pl.BlockSpec(memory_space=pl.ANY),
                  pl.BlockSpec((LANES, CMB_K), const), pl.BlockSpec((CMB_T, D), row),
                  pl.BlockSpec((None, 1, D), lambda i, *_: (mod_row(i), 0, 5)),
                  pl.BlockSpec((1, D), const), pl.BlockSpec((1, D), const)],
        out_specs=pl.BlockSpec((CMB_T, D), row),
        scratch_shapes=[pltpu.VMEM((2, CMB_K, D // 2), jnp.int32), pltpu.SemaphoreType.DMA((2,))],
    )
    return pl.pallas_call(
        _combine_kernel,
        grid_spec=grid_spec,
        out_shape=jax.ShapeDtypeStruct((M, D), F32),
        compiler_params=_params("arbitrary"),
        name="moe_combine",
    )(starts, rounds, pos, y, jnp.asarray(ex, BF16), x1, mod_l, ln_g.reshape(1, D), ln_b.reshape(1, D))


def moe_route(aff):
    pos_l, cex_l, cin_l, idx_l, tw_l = moe_select(aff, SEQ, CAP_LAT, 0, 0)
    pos_c, cex_c, cin_c, idx_c, tw_c = moe_select(aff, CTX, CAP_CTX, NL, B * CAP_LAT)
    slot_major = lambda a, c: jnp.concatenate([a.transpose(1, 0, 2).reshape(N_EXPERTS, -1),
                                               c.transpose(1, 0, 2).reshape(N_EXPERTS, -1)], axis=1)
    idx = slot_major(idx_l, idx_c).reshape(N_EXPERTS, 1, EROWS)
    tw = slot_major(tw_l, tw_c).reshape(N_EXPERTS, EROWS, 1)
    pos = jnp.concatenate([pos_l, pos_c], axis=0)
    starts = jnp.concatenate([cex_l, cex_c], axis=0)[0::CMB_T, :N_EXPERTS]
    ends = jnp.concatenate([cin_l, cin_c], axis=0)[CMB_T - 1::CMB_T, :N_EXPERTS]
    rounds = jnp.maximum(jnp.max((ends - starts + CMB_STEP - 1) // CMB_STEP, axis=1), 1)
    return idx, tw, pos, starts.reshape(-1), rounds


HY_R = 256
HY_TC = 256


def _hy_na(L):
    return 2 * L // HY_R


def _hy_tables(L):
    na = _hy_na(L)
    nk = na // 2 + 1
    eye = np.eye(8)
    ka = np.arange(nk)

    def s1(n_a):
        ang = -2.0 * np.pi * np.outer(ka, np.arange(n_a)) / na
        return np.concatenate([np.kron(np.cos(ang), eye), np.kron(np.sin(ang), eye)], axis=0)

    r = np.arange(HY_R)
    ang = -2.0 * np.pi * np.outer(r, r) / HY_R
    fs = np.concatenate([np.cos(ang), np.sin(ang)], axis=0)
    tang = -2.0 * np.pi * ka[:, None] * r[None, :] / (2 * L)
    tw_r = np.repeat(np.cos(tang)[:, :, None], LANES, axis=2)
    tw_i = np.repeat(np.sin(tang)[:, :, None], LANES, axis=2)
    wgt = np.where((ka == 0) | (ka == na // 2), 1.0, 2.0)
    iang = 2.0 * np.pi * np.outer(np.arange(na // 2), ka) / na
    g = np.concatenate([np.kron(np.cos(iang) * wgt, eye), np.kron(-np.sin(iang) * wgt, eye)], axis=1)
    f = lambda a: jnp.asarray(a, F32)
    return dict(s1_half=f(s1(na // 2)), s1_full=f(s1(na)), fs=f(fs), tw_r=f(tw_r), tw_i=f(tw_i), g=f(g))


def _hy_stage1(x_ref, s1, ar_ref, ai_ref, n_a, nk, prec):
    tc = x_ref.shape[-1]

    def body(rh, carry):
        rows = pl.ds(pl.multiple_of(rh * 8, 8), 8)
        slab = x_ref[:, rows, :].reshape(8 * n_a, tc)
        if prec is None:
            slab = slab.astype(BF16)
        out = jnp.dot(s1, slab, precision=prec, preferred_element_type=F32)
        ar_ref[:, rows, :] = out[: 8 * nk].reshape(nk, 8, tc)
        ai_ref[:, rows, :] = out[8 * nk:].reshape(nk, 8, tc)
        return carry

    lax.fori_loop(0, HY_R // 8, body, 0)


def _hy_cmul_tw(xr, xi, twr, twi, conj):
    reps = xr.shape[-1] // LANES
    if reps > 1:
        twr = jnp.concatenate([twr] * reps, axis=1)
        twi = jnp.concatenate([twi] * reps, axis=1)
    if conj:
        return xr * twr + xi * twi, xi * twr - xr * twi
    return xr * twr - xi * twi, xi * twr + xr * twi


def _hy_dft256(fs, xr, xi, inverse, prec):
    if prec is None:
        xr, xi = xr.astype(BF16), xi.astype(BF16)
    t1 = jnp.dot(fs, xr, precision=prec, preferred_element_type=F32)
    t2 = jnp.dot(fs, xi, precision=prec, preferred_element_type=F32)
    if inverse:
        return t1[:HY_R] + t2[HY_R:], t2[:HY_R] - t1[HY_R:]
    return t1[:HY_R] - t2[HY_R:], t2[:HY_R] + t1[HY_R:]


def _hy_filter_kernel(w1t_ref, w1c_ref, w1s_ref, b1_ref, w2_ref, b2_ref, w3_ref, freq_ref, dl_ref, o_ref, norm_ref, *, L):
    i = pl.program_id(0)
    tn = o_ref.shape[0]
    n = i * tn + lax.broadcasted_iota(jnp.int32, (tn, 1), 0)
    t = jnp.where(n < L, n, 2 * L - n).astype(F32)
    t_unit = t / max(L - 1, 1)
    j = lax.broadcasted_iota(jnp.int32, (1, HY_BANDS), 1).astype(F32)
    bands = 1e-4 + j * ((HY_BANDS - 1 - 1e-4) / (HY_BANDS - 1))
    ang = (2.0 * math.pi / L) * t * bands
    f = freq_ref[...]
    dot = functools.partial(jnp.dot, precision=HIGHEST, preferred_element_type=F32)
    h = t_unit * w1t_ref[...] + dot(jnp.cos(ang), w1c_ref[...]) - dot(jnp.sin(ang), w1s_ref[...])
    h = jnp.sin(f * (h + b1_ref[...]))
    h = jnp.sin(f * (dot(h, w2_ref[...]) + b2_ref[...]))
    hh = dot(h, w3_ref[...])
    half = 2 * HY_W
    k = jnp.where(n < L, hh[:, :half], hh[:, half:]) * jnp.exp(-t_unit * dl_ref[...])
    k = jnp.where(n == L, 0.0, k)
    o_ref[...] = k

    @pl.when(i == 0)
    def _():
        norm_ref[...] = jnp.zeros_like(norm_ref)

    norm_ref[...] += jnp.sum(jnp.abs(k), axis=0, keepdims=True)


def hyena_filter_time(L, w1, b1, w2, b2, w3, freq):
    tn = min(512, 2 * L)
    deltas = np.abs(np.linspace(HY_MIN_DECAY, HY_MAX_DECAY, HY_W, dtype=np.float32))
    dl = jnp.asarray(np.tile(deltas, 2)[None, :], F32)
    const = lambda shape: pl.BlockSpec(shape, lambda i: (0, 0))
    return pl.pallas_call(
        functools.partial(_hy_filter_kernel, L=L),
        grid=(2 * L // tn,),
        in_specs=[const((1, HY_HID)), const((HY_BANDS, HY_HID)), const((HY_BANDS, HY_HID)), const((1, HY_HID)),
                  const((HY_HID, HY_HID)), const((1, HY_HID)), const((HY_HID, 4 * HY_W)), const((1, HY_HID)),
                  const((1, 2 * HY_W))],
        out_specs=(pl.BlockSpec((tn, 2 * HY_W), lambda i: (i, 0)), const((1, 2 * HY_W))),
        out_shape=(jax.ShapeDtypeStruct((2 * L, 2 * HY_W), F32), jax.ShapeDtypeStruct((1, 2 * HY_W), F32)),
        compiler_params=_params("arbitrary"),
        name="hyena_filter_time",
    )(w1[0:1], w1[1:1 + HY_BANDS], w1[1 + HY_BANDS:], b1.reshape(1, -1), w2, b2.reshape(1, -1), w3,
      freq.reshape(1, -1), dl)


def _hy_spectrum_kernel(k_ref, norm_ref, s1_ref, fs_ref, twr_ref, twi_ref, kr_ref, ki_ref, ar_s, ai_s, *, L):
    ka = pl.program_id(1)
    na = _hy_na(L)
    nk = na // 2 + 1

    @pl.when(ka == 0)
    def _():
        _hy_stage1(k_ref, s1_ref[...], ar_s, ai_s, na, nk, HIGHEST)

    xr, xi = _hy_cmul_tw(ar_s[ka], ai_s[ka], twr_ref[...], twi_ref[...], False)
    br, bi = _hy_dft256(fs_ref[...], xr, xi, False, HIGHEST)
    scale = 1.0 / (2.0 * L * norm_ref[...])
    kr_ref[...] = br * scale
    ki_ref[...] = bi * scale


def hyena_filter_spectrum(L, k_time, norm, tabs):
    na = _hy_na(L)
    nk = na // 2 + 1
    tc = HY_TC
    out = jax.ShapeDtypeStruct((nk, HY_R, 2 * HY_W), F32)
    return pl.pallas_call(
        functools.partial(_hy_spectrum_kernel, L=L),
        grid=(2 * HY_W // tc, nk),
        in_specs=[pl.BlockSpec((na, HY_R, tc), lambda j, ka: (0, 0, j)),
                  pl.BlockSpec((1, tc), lambda j, ka: (0, j)),
                  pl.BlockSpec((16 * nk, 8 * na), lambda j, ka: (0, 0)),
                  pl.BlockSpec((2 * HY_R, HY_R), lambda j, ka: (0, 0)),
                  pl.BlockSpec((None, HY_R, LANES), lambda j, ka: (ka, 0, 0)),
                  pl.BlockSpec((None, HY_R, LANES), lambda j, ka: (ka, 0, 0))],
        out_specs=(pl.BlockSpec((None, HY_R, tc), lambda j, ka: (ka, 0, j)),
                   pl.BlockSpec((None, HY_R, tc), lambda j, ka: (ka, 0, j))),
        out_shape=(out, out),
        scratch_shapes=[pltpu.VMEM((nk, HY_R, tc), F32), pltpu.VMEM((nk, HY_R, tc), F32)],
        compiler_params=_params("arbitrary", "arbitrary"),
        name="hyena_filter_spectrum",
    )(k_time.reshape(na, HY_R, 2 * HY_W), norm, tabs["s1_full"], tabs["fs"], tabs["tw_r"], tabs["tw_i"])


def _hy_conv_kernel(v_ref, x1_ref, x2_ref, cw_ref, cb_ref, skip_ref, s1_ref, fs_ref, g_ref, twr_ref, twi_ref,
                    kr_ref, ki_ref, o_ref, xin_s, ar_s, ai_s, *, L):
    s = pl.program_id(2)
    na = _hy_na(L)
    nh = na // 2
    nk = nh + 1
    tc = HY_TC
    ka = s % nk

    def short_conv(ref, part):
        row = lax.broadcasted_iota(jnp.int32, (L, 1), 0)
        u = ref[...].astype(F32)
        up = jnp.where(row == 0, 0.0, pltpu.roll(u, 1, 0))
        dn = jnp.where(row == L - 1, 0.0, pltpu.roll(u, L - 1, 0))
        w = cw_ref[part]
        y = w[0:1, :] * up + w[1:2, :] * u + w[2:3, :] * dn + cb_ref[part]
        return y.reshape(nh, HY_R, tc)

    @pl.when(s == 0)
    def _():
        xin_s[...] = short_conv(v_ref, 0)

    @pl.when(ka == 0)
    def _():
        _hy_stage1(xin_s, s1_ref[...].astype(BF16), ar_s, ai_s, nh, nk, None)

    fs = fs_ref[...].astype(BF16)
    twr, twi = twr_ref[...], twi_ref[...]
    xr, xi = _hy_cmul_tw(ar_s[ka], ai_s[ka], twr, twi, False)
    br, bi = _hy_dft256(fs, xr, xi, False, None)
    kr, ki = kr_ref[...], ki_ref[...]
    pr, pi_ = br * kr - bi * ki, br * ki + bi * kr
    qr, qi = _hy_dft256(fs, pr, pi_, True, None)
    qr, qi = _hy_cmul_tw(qr, qi, twr, twi, True)
    ar_s[ka] = qr
    ai_s[ka] = qi

    def finish(order):
        gm = g_ref[...].astype(BF16)
        sk = skip_ref[order]

        def body(rh, carry):
            rows = pl.ds(pl.multiple_of(rh * 8, 8), 8)
            q = jnp.concatenate([ar_s[:, rows, :].reshape(8 * nk, tc), ai_s[:, rows, :].reshape(8 * nk, tc)], axis=0)
            y = jnp.dot(gm, q.astype(BF16), preferred_element_type=F32).reshape(nh, 8, tc)
            xin_s[:, rows, :] = y + xin_s[:, rows, :] * sk
            return carry

        lax.fori_loop(0, HY_R // 8, body, 0)

    @pl.when(s == nk - 1)
    def _():
        finish(0)
        xin_s[...] = short_conv(x1_ref, 1) * xin_s[...]

    @pl.when(s == 2 * nk - 1)
    def _():
        finish(1)
        o_ref[...] = (short_conv(x2_ref, 2) * xin_s[...]).reshape(L, tc).astype(o_ref.dtype)


def hyena_conv(p_hy, row0, L, conv_w, conv_b, skip, kf_r, kf_i, tabs):
    na = _hy_na(L)
    nh, nk = na // 2, na // 2 + 1
    tc = HY_TC
    nct = HY_W // tc
    rb0 = row0 // L
    cw = conv_w.reshape(3, 3, HY_W).transpose(1, 0, 2)
    cb = conv_b.reshape(3, 1, HY_W)
    col = lambda part: pl.BlockSpec((L, tc), lambda b, c, s, part=part: (rb0 + b, part * nct + c))
    const2 = lambda shape: pl.BlockSpec(shape, lambda b, c, s: (0, 0))
    tw_spec = pl.BlockSpec((None, HY_R, LANES), lambda b, c, s: (s % nk, 0, 0))
    kf_spec = pl.BlockSpec((None, HY_R, tc), lambda b, c, s: (s % nk, 0, (s // nk) * nct + c))
    return pl.pallas_call(
        functools.partial(_hy_conv_kernel, L=L),
        grid=(B, nct, 2 * nk),
        in_specs=[col(0), col(1), col(2),
                  pl.BlockSpec((3, 3, tc), lambda b, c, s: (0, 0, c)),
                  pl.BlockSpec((3, 1, tc), lambda b, c, s: (0, 0, c)),
                  pl.BlockSpec((2, 1, tc), lambda b, c, s: (0, 0, c)),
                  const2((16 * nk, 8 * nh)), const2((2 * HY_R, HY_R)), const2((8 * nh, 16 * nk)),
                  tw_spec, tw_spec, kf_spec, kf_spec],
        out_specs=pl.BlockSpec((L, tc), lambda b, c, s: (b, c)),
        out_shape=jax.ShapeDtypeStruct((B * L, HY_W), BF16),
        scratch_shapes=[pltpu.VMEM((nh, HY_R, tc), F32), pltpu.VMEM((nk, HY_R, tc), F32),
                        pltpu.VMEM((nk, HY_R, tc), F32)],
        compiler_params=_params("arbitrary", "arbitrary", "arbitrary"),
        name="hyena_conv_%d" % L,
    )(p_hy, p_hy, p_hy, cw, cb, skip.reshape(2, 1, HY_W), tabs["s1_half"], tabs["fs"], tabs["g"],
      tabs["tw_r"], tabs["tw_i"], kf_r, kf_i)


def hyena_branch(p_hy, row0, L, conv_w, conv_b, w1, b1, w2, b2, w3, freq, skip):
    tabs = _hy_tables(L)
    k_time, norm = hyena_filter_time(L, w1, b1, w2, b2, w3, freq)
    kf_r, kf_i = hyena_filter_spectrum(L, k_time, norm, tabs)
    return hyena_conv(p_hy, row0, L, conv_w, conv_b, skip, kf_r, kf_i, tabs)


def kernel(x, c, ctx, c_ctx, w_mod, b_mod, w_in, hy_conv_w, hy_conv_b, hy_w1, hy_b1, hy_w2, hy_b2, hy_w3, hy_freq, hy_skip, ssd_conv_w, ssd_conv_b, ssd_a_log, ssd_dt_bias, ssd_d, ssd_norm_g, da_lambda, da_subln_g, w_out, ln1_g, ln1_b, router, w_gate, w_up, w_down, ln2_g, ln2_b):
    xa = jnp.concatenate([x.reshape(NL, D), ctx.reshape(NC, D)], axis=0)
    cc = jnp.concatenate([c, c_ctx[None, :], jnp.zeros((8 - B - 1, D), F32)], axis=0)
    mod = modulation(cc, w_mod, b_mod).reshape(DEPTH, 8, 1, 6 * D)
    rope = rope_tables()
    o_ssd = HY_PROJ
    o_xbc = o_ssd + SSD_W
    o_dt = o_xbc + SSD_XBC
    o_da = o_dt + 2 * SSD_H
    w_hy = w_in[:, :, :o_ssd].astype(BF16)
    w_ssd = jnp.concatenate([w_in[:, :, o_xbc:o_dt], w_in[:, :, o_ssd:o_xbc], w_in[:, :, o_dt:o_da],
                             jnp.zeros((DEPTH, D, LANES - 2 * SSD_H), F32)], axis=2).astype(BF16)
    w_da = w_in[:, :, o_da:].astype(BF16)
    w_out_b = w_out.astype(BF16)
    r_pad = jnp.pad(router, ((0, 0), (0, 0), (0, LANES - N_EXPERTS))).astype(BF16)
    for l in range(DEPTH):
        lam_init = 0.8 - 0.6 * math.exp(-0.3 * l)
        p_hy = in_projection(xa, mod[l], w_hy, l, "hy")
        zx, dt = in_projection(xa, mod[l], w_ssd, l, "ssd")
        da = in_projection(xa, mod[l], w_da, l, "da", rope)

        hy_args = (hy_conv_w[l], hy_conv_b[l], hy_w1[l], hy_b1[l], hy_w2[l], hy_b2[l], hy_w3[l], hy_freq[l], hy_skip[l])
        y_hy = jnp.concatenate([hyena_branch(p_hy, 0, SEQ, *hy_args), hyena_branch(p_hy, NL, CTX, *hy_args)], axis=0)

        yf, yb = ssd_scan(zx, dt, ssd_conv_w[l], ssd_conv_b[l], ssd_a_log[l], ssd_dt_bias[l], ssd_d[l])
        y_ssd = ssd_gate(yf, yb, zx, ssd_norm_g[l])

        lp = da_lambda[l].astype(F32)
        lam = jnp.exp(jnp.sum(lp[0] * lp[1])) - jnp.exp(jnp.sum(lp[2] * lp[3])) + lam_init
        y_da = jnp.concatenate([diff_attention(da, lam, da_subln_g[l], lam_init, False),
                                diff_attention(da, lam, da_subln_g[l], lam_init, True)], axis=0)

        x1, h2, aff = out_projection(y_hy, y_ssd, y_da, w_out_b, l, xa, mod[l], ln1_g[l], ln1_b[l], r_pad)
        idx, tw, pos, starts, rounds = moe_route(aff)
        y = expert_ffn(idx, h2, tw, w_gate, w_up, w_down, l)
        xa = moe_combine(starts, rounds, pos, y, x1, mod[l], ln2_g[l], ln2_b[l])
    return xa[:NL].reshape(B, SEQ, D)
```

```python
import functools
import math

import numpy as np
import jax
import jax.numpy as jnp
from jax import lax
from jax.experimental import pallas as pl
from jax.experimental.pallas import tpu as pltpu

F32 = jnp.float32
BF16 = jnp.bfloat16
HIGHEST = lax.Precision.HIGHEST

D = 2048
B = 4
SEQ = 4096
DEPTH = 4
GRID_W = 64
CTX = 256
HY_W = 512
SSD_W = 512
DA_W = 1024
HY_EMB = 33
HY_BANDS = 16
HY_HID = 64
HY_MIN_DECAY = math.log(1e-2) / 1.5
HY_MAX_DECAY = math.log(1e-2) / 0.3
HY_PROJ = 3 * HY_W
SSD_H = 8
SSD_N = 128
SSD_CHUNK = 128
SSD_XBC = 1024
DA_H = 8
DA_DK = 64
DA_QK = 1024
ROPE_THETA = 10000.0
ROPE_PAIRS = 16
N_EXPERTS = 16
EC_FACTOR = 2
D_FF = 1536
ALPHA = (2 * DEPTH) ** 0.25
LN_EPS = 1e-6

NL = B * SEQ
NC = B * CTX
M = NL + NC
LANES = 128
TM = 256
VMEM_LIMIT = 56 * 1024 * 1024

N_LAT_TILES = NL // TM
TILES_PER_SAMPLE = SEQ // TM


def _mod_row(i):
    return jnp.where(i < N_LAT_TILES, i // TILES_PER_SAMPLE, B)


def _mod_spec(chunk):
    return pl.BlockSpec((None, 1, D), lambda i, chunk=chunk: (_mod_row(i), 0, chunk))


def _params(*sem):
    return pltpu.CompilerParams(dimension_semantics=sem, vmem_limit_bytes=VMEM_LIMIT)


def _ln(x):
    mu = jnp.mean(x, axis=-1, keepdims=True)
    xc = x - mu
    var = jnp.mean(xc * xc, axis=-1, keepdims=True)
    return xc * lax.rsqrt(var + LN_EPS)


def _silu(x):
    return x * jax.nn.sigmoid(x)


def _mod_kernel(c_ref, w_ref, b_ref, o_ref):
    s = _silu(c_ref[...])
    o_ref[...] = jnp.dot(s, w_ref[...], precision=HIGHEST, preferred_element_type=F32) + b_ref[...]


def modulation(cc, w_mod, b_mod):
    tn = 1024
    return pl.pallas_call(
        _mod_kernel,
        grid=(DEPTH, 6 * D // tn),
        in_specs=[
            pl.BlockSpec((8, D), lambda l, j: (0, 0)),
            pl.BlockSpec((None, D, tn), lambda l, j: (l, 0, j)),
            pl.BlockSpec((None, 1, tn), lambda l, j: (l, 0, j)),
        ],
        out_specs=pl.BlockSpec((None, 8, tn), lambda l, j: (l, 0, j)),
        out_shape=jax.ShapeDtypeStruct((DEPTH, 8, 6 * D), F32),
        compiler_params=_params("arbitrary", "arbitrary"),
        name="modulation",
    )(cc, w_mod, b_mod.reshape(DEPTH, 1, 6 * D))


def _inproj_kernel(x_ref, sh_ref, sc_ref, w_ref, *rest, mode):
    h = _ln(x_ref[...]) * (1.0 + sc_ref[...]) + sh_ref[...]
    acc = jnp.dot(h.astype(BF16), w_ref[...], preferred_element_type=F32)
    if mode == "hy":
        (o_ref,) = rest
        o_ref[...] = acc.astype(o_ref.dtype)
    elif mode == "ssd":
        o_ref, dt_ref = rest
        o_ref[...] = acc[:, : SSD_XBC + SSD_W].astype(o_ref.dtype)
        dt_ref[...] = acc[:, SSD_XBC + SSD_W:]
    else:
        cos_ref, sin_ref, o_ref = rest
        cs = cos_ref[...]
        sn = sin_ref[...]
        lane = lax.broadcasted_iota(jnp.int32, (TM, LANES), 1)
        is_a = (lane % 32) < ROPE_PAIRS
        for g in range(2 * DA_QK // LANES):
            blk = acc[:, g * LANES:(g + 1) * LANES]
            swapped = jnp.where(is_a, pltpu.roll(blk, LANES - ROPE_PAIRS, 1), pltpu.roll(blk, ROPE_PAIRS, 1))
            rot = blk * cs + swapped * sn
            if g < DA_QK // LANES:
                rot = rot * (DA_DK ** -0.5 * math.log2(math.e))
            o_ref[:, g * LANES:(g + 1) * LANES] = rot.astype(o_ref.dtype)
        o_ref[:, 2 * DA_QK:] = acc[:, 2 * DA_QK:].astype(o_ref.dtype)


def in_projection(x, mod_l, w, l, mode, rope=None):
    n = w.shape[2]
    in_specs = [
        pl.BlockSpec((TM, D), lambda i: (i, 0)),
        _mod_spec(0),
        _mod_spec(1),
        pl.BlockSpec((None, D, n), lambda i: (l, 0, 0)),
    ]
    args = [x, mod_l, mod_l, w]
    if mode == "hy":
        out_shape = jax.ShapeDtypeStruct((M, n), BF16)
        out_specs = pl.BlockSpec((TM, n), lambda i: (i, 0))
    elif mode == "ssd":
        n0 = SSD_XBC + SSD_W
        out_shape = (jax.ShapeDtypeStruct((M, n0), BF16), jax.ShapeDtypeStruct((M, LANES), F32))
        out_specs = (pl.BlockSpec((TM, n0), lambda i: (i, 0)), pl.BlockSpec((TM, LANES), lambda i: (i, 0)))
    else:
        rope_idx = lambda i: (jnp.where(i < N_LAT_TILES, i % TILES_PER_SAMPLE, TILES_PER_SAMPLE), 0)
        in_specs += [pl.BlockSpec((TM, LANES), rope_idx), pl.BlockSpec((TM, LANES), rope_idx)]
        args += list(rope)
        out_shape = jax.ShapeDtypeStruct((M, n), BF16)
        out_specs = pl.BlockSpec((TM, n), lambda i: (i, 0))
    return pl.pallas_call(
        functools.partial(_inproj_kernel, mode=mode),
        grid=(M // TM,),
        in_specs=in_specs,
        out_specs=out_specs,
        out_shape=out_shape,
        compiler_params=_params("arbitrary"),
        name="inproj_" + mode,
    )(*args)


def rope_tables():
    pos = np.arange(SEQ)
    r = (pos // GRID_W).astype(np.float64)
    col = (pos % GRID_W).astype(np.float64)
    inv = ROPE_THETA ** (-np.arange(ROPE_PAIRS, dtype=np.float64) / ROPE_PAIRS)
    inv = inv.astype(np.float32).astype(np.float64)
    lane = np.arange(LANES)
    d = lane % DA_DK
    axis = d // 32
    pair = d % ROPE_PAIRS
    is_b = (d % 32) >= ROPE_PAIRS
    ang = np.where(axis[None, :] == 0, r[:, None], col[:, None]) * inv[pair][None, :]
    cos = np.cos(ang)
    sin = np.where(is_b[None, :], np.sin(ang), -np.sin(ang))
    cos = np.concatenate([cos, np.ones((TM, LANES))], axis=0)
    sin = np.concatenate([sin, np.zeros((TM, LANES))], axis=0)
    return jnp.asarray(cos, F32), jnp.asarray(sin, F32)


ATT_SUB = 256


def _attn_kernel(lam_ref, q_ref, *rest, n_seg, post_scale):
    kv = rest[: 2 * n_seg]
    g_ref, o_ref = rest[2 * n_seg:]
    lam = lam_ref[0]
    nt = (((1,), (1,)), ((), ()))
    for sub in range(q_ref.shape[0] // ATT_SUB):
        rows = slice(sub * ATT_SUB, (sub + 1) * ATT_SUB)
        q = q_ref[rows, :]
        lane = lax.broadcasted_iota(jnp.int32, q.shape, 1)
        zero = jnp.zeros_like(q)
        qm = (jnp.where(lane < DA_DK, q, zero), jnp.where(lane >= DA_DK, q, zero))
        scores = [[lax.dot_general(qm[m], kv[2 * s][...], nt, preferred_element_type=F32) for s in range(n_seg)]
                  for m in range(2)]
        probs = []
        for m in range(2):
            mx = functools.reduce(jnp.maximum, [jnp.max(s, axis=-1, keepdims=True) for s in scores[m]])
            es = [jnp.exp2(s - mx) for s in scores[m]]
            tot = functools.reduce(lambda a, b: a + b, [jnp.sum(e, axis=-1, keepdims=True) for e in es])
            probs.append((es, 1.0 / tot))
        out = None
        for s in range(n_seg):
            a = probs[0][0][s] * probs[0][1] - (lam * probs[1][1]) * probs[1][0][s]
            o = jnp.dot(a.astype(BF16), kv[2 * s + 1][...], preferred_element_type=F32)
            out = o if out is None else out + o
        out = out * lax.rsqrt(jnp.mean(out * out, axis=-1, keepdims=True) + LN_EPS)
        o_ref[rows, :] = ((out * g_ref[...]) * post_scale).astype(o_ref.dtype)


def diff_attention(da, lam, subln_g, lam_init, ctx_queries):
    tq = CTX if ctx_queries else 2 * ATT_SUB
    lam = jnp.reshape(lam, (1,)).astype(F32)
    g = subln_g.reshape(1, LANES)
    kc_spec = pl.BlockSpec((CTX, LANES), lambda b, h, i: (NL // CTX + b, DA_H + h))
    vc_spec = pl.BlockSpec((CTX, LANES), lambda b, h, i: (NL // CTX + b, 2 * DA_H + h))
    if ctx_queries:
        rows, nq = NC, CTX // tq
        q_spec = pl.BlockSpec((tq, LANES), lambda b, h, i: ((NL + b * CTX) // tq + i, h))
        segs, seg_args = [kc_spec, vc_spec], [da, da]
    else:
        rows, nq = NL, SEQ // tq
        q_spec = pl.BlockSpec((tq, LANES), lambda b, h, i: (b * nq + i, h))
        kl_spec = pl.BlockSpec((SEQ, LANES), lambda b, h, i: (b, DA_H + h))
        vl_spec = pl.BlockSpec((SEQ, LANES), lambda b, h, i: (b, 2 * DA_H + h))
        segs, seg_args = [kl_spec, vl_spec, kc_spec, vc_spec], [da, da, da, da]
    return pl.pallas_call(
        functools.partial(_attn_kernel, n_seg=len(segs) // 2, post_scale=1.0 - lam_init),
        grid=(B, DA_H, nq),
        in_specs=[pl.BlockSpec(memory_space=pltpu.SMEM), q_spec] + segs + [pl.BlockSpec((1, LANES), lambda b, h, i: (0, 0))],
        out_specs=pl.BlockSpec((tq, LANES), lambda b, h, i: (b * nq + i, h)),
        out_shape=jax.ShapeDtypeStruct((rows, DA_W), BF16),
        compiler_params=_params("arbitrary", "arbitrary", "arbitrary"),
        name="diff_attn_ctx" if ctx_queries else "diff_attn_lat",
    )(lam, da, *seg_args, g)


SSD_NC_CTX = CTX // SSD_CHUNK
SSD_NC_LAT = SEQ // SSD_CHUNK
SSD_STEPS = SSD_NC_CTX + SSD_NC_LAT


def _ssd_chunk(s, direction):
    in_ctx = s < SSD_NC_CTX
    k = jnp.where(in_ctx, s, s - SSD_NC_CTX)
    nck = jnp.where(in_ctx, SSD_NC_CTX, SSD_NC_LAT)
    if direction == 1:
        k = nck - 1 - k
    return in_ctx, k, nck


def _ssd_rowblock(b, s, direction):
    in_ctx, k, _ = _ssd_chunk(s, direction)
    return jnp.where(in_ctx, NL // SSD_CHUNK + b * SSD_NC_CTX + k, b * SSD_NC_LAT + k)


def _ssd_kernel(xf_ref, xfp_ref, xfn_ref, dtf_ref, xb_ref, xbp_ref, xbn_ref, dtb_ref,
                cw_ref, cb_ref, a_ref, dtbias_ref, dskip_ref, yf_ref, yb_ref, state_ref):
    s = pl.program_id(1)
    q = SSD_CHUNK

    @pl.when(s == 0)
    def _():
        state_ref[...] = jnp.zeros_like(state_ref)

    row = lax.broadcasted_iota(jnp.int32, (q, 1), 0)
    ri = lax.broadcasted_iota(jnp.int32, (q, q), 0)
    ci = lax.broadcasted_iota(jnp.int32, (q, q), 1)
    lane_lo = lax.broadcasted_iota(jnp.int32, (1, LANES), 1) < 64
    cw = cw_ref[...]
    dirs = ((xf_ref, xfp_ref, xfn_ref, dtf_ref, yf_ref), (xb_ref, xbp_ref, xbn_ref, dtb_ref, yb_ref))
    for d, (x_ref, xp_ref, xn_ref, dt_ref, y_ref) in enumerate(dirs):
        _, k, nck = _ssd_chunk(s, d)
        u = x_ref[...].astype(F32)
        prev = jnp.where(k == 0, 0.0, xp_ref[15:16, :].astype(F32))
        nxt = jnp.where(k == nck - 1, 0.0, xn_ref[0:1, :].astype(F32))
        up = jnp.where(row == 0, prev, pltpu.roll(u, 1, 0))
        dn = jnp.where(row == q - 1, nxt, pltpu.roll(u, q - 1, 0))
        conv = cw[0:1, :] * up + cw[1:2, :] * u + cw[2:3, :] * dn + cb_ref[...]
        xbc = _silu(conv)
        xs = xbc[:, :SSD_W]
        bm = xbc[:, SSD_W:SSD_W + 2 * SSD_N]
        cm = xbc[:, SSD_W + 2 * SSD_N:]
        z = dt_ref[...] + dtbias_ref[...]
        dt = jnp.maximum(z, 0.0) + jnp.log(1.0 + jnp.exp(-jnp.abs(z)))
        a = dt * a_ref[...]
        tri = (ci <= ri) if d == 0 else (ci >= ri)
        a_cum = jnp.dot(tri.astype(F32), a, precision=HIGHEST, preferred_element_type=F32)
        a_cum_t = a_cum.T
        ea = jnp.exp(a_cum)
        a_tot = a_cum[q - 1:q, :] if d == 0 else a_cum[0:1, :]
        to_end = jnp.exp(a_tot - a_cum)
        chunk_decay = jnp.exp(a_tot)

        def halves(col_arr, c0, c1):
            return jnp.where(lane_lo, col_arr[:, c0:c0 + 1], col_arr[:, c1:c1 + 1])

        for g in range(2):
            bg = bm[:, g * SSD_N:(g + 1) * SSD_N]
            cgb = cm[:, g * SSD_N:(g + 1) * SSD_N].astype(BF16)
            gmat = lax.dot_general(cgb, bg.astype(BF16), (((1,), (1,)), ((), ())), preferred_element_type=F32)
            bgt = bg.T.astype(BF16)
            for pr in range(2):
                p = 2 * g + pr
                c0, c1 = 8 * d + 2 * p, 8 * d + 2 * p + 1
                xpair = xs[:, p * LANES:(p + 1) * LANES]
                xdt = xpair * halves(dt, c0, c1)
                y = None
                for hh, hc in ((0, c0), (1, c1)):
                    seg = a_cum[:, hc:hc + 1] - a_cum_t[hc:hc + 1, :]
                    lmat = jnp.exp(jnp.where(tri, seg, -jnp.inf))
                    xm = jnp.where(lane_lo if hh == 0 else jnp.logical_not(lane_lo), xdt, 0.0)
                    t = jnp.dot((gmat * lmat).astype(BF16), xm.astype(BF16), preferred_element_type=F32)
                    y = t if y is None else y + t
                st = state_ref[d, p]
                y = y + halves(ea, c0, c1) * jnp.dot(cgb, st.astype(BF16), preferred_element_type=F32)
                upd = jnp.dot(bgt, (xdt * halves(to_end, c0, c1)).astype(BF16), preferred_element_type=F32)
                state_ref[d, p] = halves(chunk_decay, c0, c1) * st + upd
                if d == 0:
                    y = y + dskip_ref[:, p * LANES:(p + 1) * LANES] * xpair
                y_ref[:, p * LANES:(p + 1) * LANES] = y


def ssd_scan(zx, dt, conv_w, conv_b, a_log, dt_bias, d_skip):
    n16 = M // 16

    def chunk_spec(direction):
        return pl.BlockSpec((SSD_CHUNK, SSD_XBC), lambda b, s: (_ssd_rowblock(b, s, direction), 0))

    def prev_spec(direction):
        return pl.BlockSpec((16, SSD_XBC), lambda b, s: (jnp.maximum(_ssd_rowblock(b, s, direction) * 8 - 1, 0), 0))

    def next_spec(direction):
        return pl.BlockSpec((16, SSD_XBC), lambda b, s: (jnp.minimum(_ssd_rowblock(b, s, direction) * 8 + 8, n16 - 1), 0))

    def dt_spec(direction):
        return pl.BlockSpec((SSD_CHUNK, LANES), lambda b, s: (_ssd_rowblock(b, s, direction), 0))

    def y_spec(direction):
        return pl.BlockSpec((SSD_CHUNK, SSD_W), lambda b, s: (_ssd_rowblock(b, s, direction), 0))

    const = lambda shape: pl.BlockSpec(shape, lambda b, s: (0, 0))
    pad16 = lambda v: jnp.pad(v.reshape(1, 2 * SSD_H).astype(F32), ((0, 0), (0, LANES - 2 * SSD_H)))
    a_row = pad16(-jnp.exp(a_log.astype(F32)))
    bias_row = pad16(dt_bias)
    dskip_row = jnp.repeat(d_skip.astype(F32), SSD_W // SSD_H).reshape(1, SSD_W)
    return pl.pallas_call(
        _ssd_kernel,
        grid=(B, SSD_STEPS),
        in_specs=[chunk_spec(0), prev_spec(0), next_spec(0), dt_spec(0),
                  chunk_spec(1), prev_spec(1), next_spec(1), dt_spec(1),
                  const((3, SSD_XBC)), const((1, SSD_XBC)), const((1, LANES)), const((1, LANES)), const((1, SSD_W))],
        out_specs=(y_spec(0), y_spec(1)),
        out_shape=(jax.ShapeDtypeStruct((M, SSD_W), F32), jax.ShapeDtypeStruct((M, SSD_W), F32)),
        scratch_shapes=[pltpu.VMEM((2, 4, SSD_N, LANES), F32)],
        compiler_params=_params("arbitrary", "arbitrary"),
        name="ssd_scan",
    )(zx, zx, zx, dt, zx, zx, zx, dt, conv_w, conv_b.reshape(1, SSD_XBC), a_row, bias_row, dskip_row)


def _ssd_gate_kernel(yf_ref, yb_ref, z_ref, g_ref, o_ref):
    y = (yf_ref[...] + yb_ref[...]) * _silu(z_ref[...].astype(F32))
    y = y * lax.rsqrt(jnp.mean(y * y, axis=-1, keepdims=True) + LN_EPS)
    o_ref[...] = (y * g_ref[...]).astype(o_ref.dtype)


def ssd_gate(yf, yb, zx, norm_g):
    tm = 512
    row = lambda i: (i, 0)
    return pl.pallas_call(
        _ssd_gate_kernel,
        grid=(M // tm,),
        in_specs=[pl.BlockSpec((tm, SSD_W), row), pl.BlockSpec((tm, SSD_W), row),
                  pl.BlockSpec((tm, SSD_W), lambda i: (i, SSD_XBC // SSD_W)),
                  pl.BlockSpec((1, SSD_W), lambda i: (0, 0))],
        out_specs=pl.BlockSpec((tm, SSD_W), row),
        out_shape=jax.ShapeDtypeStruct((M, SSD_W), BF16),
        compiler_params=_params("arbitrary"),
        name="ssd_gate",
    )(yf, yb, zx, norm_g.reshape(1, SSD_W))


def _pack_halves(v):
    n = v.shape[-1] // 2
    lo = lax.shift_right_logical(lax.bitcast_convert_type(v[:, :n], jnp.int32), 16)
    hi = lax.bitcast_convert_type(v[:, n:], jnp.int32) & jnp.int32(-65536)
    return hi | lo


def _unpack_halves(w):
    lo = lax.bitcast_convert_type(lax.shift_left(w, 16), F32).astype(BF16)
    hi = lax.bitcast_convert_type(w & jnp.int32(-65536), F32).astype(BF16)
    return lo, hi


def _outproj_kernel(yh_ref, ys_ref, ya_ref, w_ref, x_ref, g1_ref, lng_ref, lnb_ref, sh2_ref, sc2_ref, r_ref,
                    x1_ref, h2_ref, aff_ref):
    y = jnp.dot(yh_ref[...], w_ref[0:HY_W, :], preferred_element_type=F32)
    y = y + jnp.dot(ys_ref[...], w_ref[HY_W:HY_W + SSD_W, :], preferred_element_type=F32)
    y = y + jnp.dot(ya_ref[...], w_ref[HY_W + SSD_W:, :], preferred_element_type=F32)
    x1 = _ln(ALPHA * x_ref[...] + g1_ref[...] * y) * lng_ref[...] + lnb_ref[...]
    x1_ref[...] = x1
    h2 = (_ln(x1) * (1.0 + sc2_ref[...]) + sh2_ref[...]).astype(BF16)
    h2_ref[...] = _pack_halves(h2.astype(F32))
    logits = jnp.dot(h2, r_ref[...], preferred_element_type=F32)
    lane = lax.broadcasted_iota(jnp.int32, logits.shape, 1)
    logits = jnp.where(lane < N_EXPERTS, logits, -jnp.inf)
    e = jnp.exp(logits - jnp.max(logits, axis=-1, keepdims=True))
    aff_ref[...] = e / jnp.sum(e, axis=-1, keepdims=True)


def out_projection(yh, ys, ya, w_out, l, x, mod_l, ln_g, ln_b, r_pad):
    row = lambda i: (i, 0)
    const = lambda i: (0, 0)
    layer = lambda i: (l, 0, 0)
    return pl.pallas_call(
        _outproj_kernel,
        grid=(M // TM,),
        in_specs=[pl.BlockSpec((TM, HY_W), row), pl.BlockSpec((TM, SSD_W), row), pl.BlockSpec((TM, DA_W), row),
                  pl.BlockSpec((None, D, D), layer), pl.BlockSpec((TM, D), row), _mod_spec(2),
                  pl.BlockSpec((1, D), const), pl.BlockSpec((1, D), const), _mod_spec(3), _mod_spec(4),
                  pl.BlockSpec((None, D, LANES), layer)],
        out_specs=(pl.BlockSpec((TM, D), row), pl.BlockSpec((TM, D // 2), row), pl.BlockSpec((TM, LANES), row)),
        out_shape=(jax.ShapeDtypeStruct((M, D), F32), jax.ShapeDtypeStruct((M, D // 2), jnp.int32),
                   jax.ShapeDtypeStruct((M, LANES), F32)),
        compiler_params=_params("arbitrary"),
        name="outproj",
    )(yh, ys, ya, w_out, x, mod_l, ln_g.reshape(1, D), ln_b.reshape(1, D), mod_l, mod_l, r_pad)


CAP_LAT = EC_FACTOR * SEQ // N_EXPERTS
CAP_CTX = EC_FACTOR * CTX // N_EXPERTS
EROWS = B * (CAP_LAT + CAP_CTX)
FFN_TR = EROWS // 2
FFN_TF = 256
FFN_NF = D_FF // FFN_TF
FFN_STEPS = (EROWS // FFN_TR) * FFN_NF
FFN_GCHUNK = -(-EROWS // FFN_STEPS // 8) * 8
FFN_GROWS = FFN_STEPS * FFN_GCHUNK


def _gather_copy(h2_hbm, raw_s, sem, src_row, dst_row):
    return pltpu.make_async_copy(h2_hbm.at[pl.ds(src_row, 1), :], raw_s.at[pl.ds(dst_row, 1), :], sem)


def _ffn_kernel(idx_ref, idxn_ref, h2_hbm, tw_ref, wg_ref, wu_ref, wd_ref, o_ref, raw_s, xlo_s, xhi_s, acc_s, sem):
    e, r, f = pl.program_id(0), pl.program_id(1), pl.program_id(2)
    half = D // 2
    k = r * FFN_NF + f

    def wait_gather():
        def body(i, carry):
            _gather_copy(h2_hbm, raw_s, sem, 0, i).wait()
            return carry
        lax.fori_loop(0, FFN_GROWS, body, 0, unroll=16)

    @pl.when(k == 0)
    def _():
        @pl.when(e == 0)
        def _():
            def body(i, carry):
                _gather_copy(h2_hbm, raw_s, sem, idx_ref[0, i], i).start()
                return carry
            lax.fori_loop(0, FFN_GROWS, body, 0, unroll=8)

        wait_gather()
        lo, hi = _unpack_halves(raw_s[0:EROWS, :])
        xlo_s[...] = lo
        xhi_s[...] = hi

    base = k * FFN_GCHUNK
    for j in range(FFN_GCHUNK):
        _gather_copy(h2_hbm, raw_s, sem, idxn_ref[0, base + j], base + j).start()

    rows = pl.ds(pl.multiple_of(r * FFN_TR, 16), FFN_TR)
    xlo = xlo_s[rows, :]
    xhi = xhi_s[rows, :]

    def proj(w_ref):
        w = w_ref[...].astype(BF16)
        return (jnp.dot(xlo, w[:half], preferred_element_type=F32) + jnp.dot(xhi, w[half:], preferred_element_type=F32))

    g = proj(wg_ref)
    u = proj(wu_ref)
    t = jnp.dot((_silu(g) * u).astype(BF16), wd_ref[...].astype(BF16), preferred_element_type=F32)

    @pl.when(f == 0)
    def _():
        acc_s[...] = t

    @pl.when(f != 0)
    def _():
        acc_s[...] += t

    @pl.when(f == FFN_NF - 1)
    def _():
        y = (acc_s[...] * tw_ref[...]).astype(BF16).astype(F32)
        o_ref[...] = _pack_halves(y)

    @pl.when((e == N_EXPERTS - 1) & (k == FFN_STEPS - 1))
    def _():
        wait_gather()


def expert_ffn(idx, h2p, tw, w_gate, w_up, w_down, l):
    idx = jnp.pad(idx, ((0, 0), (0, 0), (0, FFN_GROWS - EROWS)))
    smem_idx = lambda shift: pl.BlockSpec((None, 1, FFN_GROWS), lambda e, r, f: (jnp.minimum(e + shift, N_EXPERTS - 1), 0, 0),
                                          memory_space=pltpu.SMEM)
    return pl.pallas_call(
        _ffn_kernel,
        grid=(N_EXPERTS, EROWS // FFN_TR, FFN_NF),
        in_specs=[smem_idx(0), smem_idx(1),
                  pl.BlockSpec(memory_space=pl.ANY),
                  pl.BlockSpec((None, FFN_TR, 1), lambda e, r, f: (e, r, 0)),
                  pl.BlockSpec((None, None, D, FFN_TF), lambda e, r, f: (l, e, 0, f)),
                  pl.BlockSpec((None, None, D, FFN_TF), lambda e, r, f: (l, e, 0, f)),
                  pl.BlockSpec((None, None, FFN_TF, D), lambda e, r, f: (l, e, f, 0))],
        out_specs=pl.BlockSpec((None, FFN_TR, D // 2), lambda e, r, f: (e, r, 0)),
        out_shape=jax.ShapeDtypeStruct((N_EXPERTS, EROWS, D // 2), jnp.int32),
        scratch_shapes=[pltpu.VMEM((FFN_GROWS, D // 2), jnp.int32), pltpu.VMEM((EROWS, D // 2), BF16),
                        pltpu.VMEM((EROWS, D // 2), BF16), pltpu.VMEM((FFN_TR, D), F32),
                        pltpu.SemaphoreType.DMA(())],
        compiler_params=_params("arbitrary", "arbitrary", "arbitrary"),
        name="expert_ffn",
    )(idx, idx, h2p, tw, w_gate, w_up, w_down)


SEL_BLK = 256


def _select_kernel(aff_ref, pos_ref, cex_ref, cin_ref, idx_ref, tw_ref, *, n, cap, row_base, slot_base):
    b = pl.program_id(0)
    aff = aff_ref[...]
    bits = lax.bitcast_convert_type(aff, jnp.int32)

    def search(i, thr):
        cand = thr | jnp.left_shift(jnp.int32(1), 30 - i)
        cnt = jnp.sum((bits >= cand).astype(jnp.int32), axis=0, keepdims=True)
        return jnp.where(cnt >= cap, cand, thr)

    thr = lax.fori_loop(0, 31, search, jnp.zeros((1, LANES), jnp.int32))
    gt = bits > thr
    eq = bits == thr
    need = (cap - jnp.sum(gt.astype(jnp.int32), axis=0, keepdims=True)).astype(F32)
    ri = lax.broadcasted_iota(jnp.int32, (SEL_BLK, SEL_BLK), 0)
    ci = lax.broadcasted_iota(jnp.int32, (SEL_BLK, SEL_BLK), 1)
    tri = (ci <= ri).astype(BF16)

    def prefix(mask):
        m = mask.astype(BF16)
        off = jnp.zeros((1, LANES), F32)
        parts = []
        for k in range(n // SEL_BLK):
            p = jnp.dot(tri, m[k * SEL_BLK:(k + 1) * SEL_BLK], preferred_element_type=F32) + off
            parts.append(p)
            off = p[SEL_BLK - 1:SEL_BLK, :]
        return jnp.concatenate(parts, axis=0) if len(parts) > 1 else parts[0]

    sel = gt | (eq & (prefix(eq) <= need))
    cnt = prefix(sel)
    base = slot_base + b * cap
    key = jnp.where(sel, cnt - 1.0, -1.0)
    pos_ref[...] = jnp.where(sel, cnt.astype(jnp.int32) - 1 + base, -1)
    cin_ref[...] = cnt.astype(jnp.int32) + base
    cex_ref[...] = cnt.astype(jnp.int32) - sel.astype(jnp.int32) + base
    slots = lax.broadcasted_iota(jnp.int32, (1, cap), 1).astype(F32)
    tok = (lax.broadcasted_iota(jnp.int32, (n, 1), 0) + (row_base + b * n)).astype(F32)
    for e in range(N_EXPERTS):
        hit = key[:, e:e + 1] == slots
        idx_ref[e:e + 1, :] = jnp.sum(jnp.where(hit, tok, 0.0), axis=0, keepdims=True).astype(jnp.int32)
        tw_ref[e:e + 1, :] = jnp.sum(jnp.where(hit, aff[:, e:e + 1], 0.0), axis=0, keepdims=True)


def moe_select(aff, n, cap, row_base, slot_base):
    rb = row_base // n
    tok = jax.ShapeDtypeStruct((B * n, LANES), jnp.int32)
    tok_spec = pl.BlockSpec((n, LANES), lambda b: (b, 0))
    return pl.pallas_call(
        functools.partial(_select_kernel, n=n, cap=cap, row_base=row_base, slot_base=slot_base),
        grid=(B,),
        in_specs=[pl.BlockSpec((n, LANES), lambda b: (rb + b, 0))],
        out_specs=(tok_spec, tok_spec, tok_spec,
                   pl.BlockSpec((None, N_EXPERTS, cap), lambda b: (b, 0, 0)),
                   pl.BlockSpec((None, N_EXPERTS, cap), lambda b: (b, 0, 0))),
        out_shape=(tok, tok, tok, jax.ShapeDtypeStruct((B, N_EXPERTS, cap), jnp.int32),
                   jax.ShapeDtypeStruct((B, N_EXPERTS, cap), F32)),
        compiler_params=_params("arbitrary"),
        name="moe_select_%d" % n,
    )(aff)


CMB_T = 128
CMB_J = 40
CMB_STEP = CMB_J - 8
CMB_K = N_EXPERTS * CMB_J
CMB_TILES = M // CMB_T


def _window_start(lo):
    return jnp.minimum((lo // 8) * 8, EROWS - CMB_J)


def _combine_kernel(starts_ref, nr_ref, pos_ref, y_hbm, ex_ref, x1_ref, g2_ref, lng_ref, lnb_ref, o_ref, wbuf, sem):
    i = pl.program_id(0)
    slot = i % 2

    def copies(tile, rnd, slot_):
        out = []
        for e in range(N_EXPERTS):
            st = _window_start(starts_ref[tile * N_EXPERTS + e] + rnd * CMB_STEP)
            out.append(pltpu.make_async_copy(y_hbm.at[e, pl.ds(pl.multiple_of(st, 8), CMB_J), :],
                                             wbuf.at[slot_, pl.ds(e * CMB_J, CMB_J), :], sem.at[slot_]))
        return out

    @pl.when(i == 0)
    def _():
        for cp in copies(0, 0, 0):
            cp.start()

    @pl.when(i + 1 < CMB_TILES)
    def _():
        for cp in copies(i + 1, 0, 1 - slot):
            cp.start()

    lane = lax.broadcasted_iota(jnp.int32, (1, LANES), 1)
    jmod = (lax.broadcasted_iota(jnp.int32, (1, CMB_K), 1) % CMB_J).astype(F32)
    pos = pos_ref[...]

    def scatter_round(rnd):
        lov = jnp.zeros((1, LANES), jnp.int32)
        for e in range(N_EXPERTS):
            lov = jnp.where(lane == e, starts_ref[i * N_EXPERTS + e] + rnd * CMB_STEP, lov)
        stv = _window_start(lov)
        valid = (pos >= lov) & (pos < lov + CMB_STEP) & (lane < N_EXPERTS)
        rel = jnp.where(valid, pos - stv, -1).astype(F32).astype(BF16)
        spread = jnp.dot(rel, ex_ref[...], preferred_element_type=F32)
        onehot = (spread == jmod).astype(BF16)
        lo, hi = _unpack_halves(wbuf[slot])
        return jnp.concatenate([jnp.dot(onehot, lo, preferred_element_type=F32),
                                jnp.dot(onehot, hi, preferred_element_type=F32)], axis=1)

    for cp in copies(i, 0, slot):
        cp.wait()
    moe = scatter_round(0)

    def extra(rnd, acc):
        for cp in copies(i, rnd, slot):
            cp.start()
        for cp in copies(i, rnd, slot):
            cp.wait()
        return acc + scatter_round(rnd)

    moe = lax.fori_loop(1, nr_ref[i], extra, moe)
    o_ref[...] = _ln(ALPHA * x1_ref[...] + g2_ref[...] * moe) * lng_ref[...] + lnb_ref[...]


def moe_combine(starts, rounds, pos, y, x1, mod_l, ln_g, ln_b):
    ex = np.zeros((LANES, CMB_K), np.float32)
    for e in range(N_EXPERTS):
        ex[e, e * CMB_J:(e + 1) * CMB_J] = 1.0
    lat_tiles = NL // CMB_T
    mod_row = lambda i: jnp.where(i < lat_tiles, i // (SEQ // CMB_T), B)
    row = lambda i, *_: (i, 0)
    const = lambda i, *_: (0, 0)
    grid_spec = pltpu.PrefetchScalarGridSpec(
        num_scalar_prefetch=2,
        grid=(CMB_TILES,),
        in_specs=[pl.BlockSpec((CMB_T, LANES), row), pl.BlockSpec(memory_space=pl.ANY),
                  pl.BlockSpec((LANES, CMB_K), const), pl.BlockSpec((CMB_T, D), row),
                  pl.BlockSpec((None, 1, D), lambda i, *_: (mod_row(i), 0, 5)),
                  pl.BlockSpec((1, D), const), pl.BlockSpec((1, D), const)],
        out_specs=pl.BlockSpec((CMB_T, D), row),
        scratch_shapes=[pltpu.VMEM((2, CMB_K, D // 2), jnp.int32), pltpu.SemaphoreType.DMA((2,))],
    )
    return pl.pallas_call(
        _combine_kernel,
        grid_spec=grid_spec,
        out_shape=jax.ShapeDtypeStruct((M, D), F32),
        compiler_params=_params("arbitrary"),
        name="moe_combine",
    )(starts, rounds, pos, y, jnp.asarray(ex, BF16), x1, mod_l, ln_g.reshape(1, D), ln_b.reshape(1, D))


def moe_route(aff):
    pos_l, cex_l, cin_l, idx_l, tw_l = moe_select(aff, SEQ, CAP_LAT, 0, 0)
    pos_c, cex_c, cin_c, idx_c, tw_c = moe_select(aff, CTX, CAP_CTX, NL, B * CAP_LAT)
    slot_major = lambda a, c: jnp.concatenate([a.transpose(1, 0, 2).reshape(N_EXPERTS, -1),
                                               c.transpose(1, 0, 2).reshape(N_EXPERTS, -1)], axis=1)
    idx = slot_major(idx_l, idx_c).reshape(N_EXPERTS, 1, EROWS)
    tw = slot_major(tw_l, tw_c).reshape(N_EXPERTS, EROWS, 1)
    pos = jnp.concatenate([pos_l, pos_c], axis=0)
    starts = jnp.concatenate([cex_l, cex_c], axis=0)[0::CMB_T, :N_EXPERTS]
    ends = jnp.concatenate([cin_l, cin_c], axis=0)[CMB_T - 1::CMB_T, :N_EXPERTS]
    rounds = jnp.maximum(jnp.max((ends - starts + CMB_STEP - 1) // CMB_STEP, axis=1), 1)
    return idx, tw, pos, starts.reshape(-1), rounds


HY_R = 256
HY_TC = 256


def _hy_na(L):
    return 2 * L // HY_R


def _hy_tables(L):
    na = _hy_na(L)
    nk = na // 2 + 1
    eye = np.eye(8)
    ka = np.arange(nk)

    def s1(n_a):
        ang = -2.0 * np.pi * np.outer(ka, np.arange(n_a)) / na
        return np.concatenate([np.kron(np.cos(ang), eye), np.kron(np.sin(ang), eye)], axis=0)

    r = np.arange(HY_R)
    ang = -2.0 * np.pi * np.outer(r, r) / HY_R
    fs = np.concatenate([np.cos(ang), np.sin(ang)], axis=0)
    tang = -2.0 * np.pi * ka[:, None] * r[None, :] / (2 * L)
    tw_r = np.repeat(np.cos(tang)[:, :, None], LANES, axis=2)
    tw_i = np.repeat(np.sin(tang)[:, :, None], LANES, axis=2)
    wgt = np.where((ka == 0) | (ka == na // 2), 1.0, 2.0)
    iang = 2.0 * np.pi * np.outer(np.arange(na // 2), ka) / na
    g = np.concatenate([np.kron(np.cos(iang) * wgt, eye), np.kron(-np.sin(iang) * wgt, eye)], axis=1)
    f = lambda a: jnp.asarray(a, F32)
    return dict(s1_half=f(s1(na // 2)), s1_full=f(s1(na)), fs=f(fs), tw_r=f(tw_r), tw_i=f(tw_i), g=f(g))


def _hy_stage1(x_ref, s1, ar_ref, ai_ref, n_a, nk, prec):
    tc = x_ref.shape[-1]

    def body(rh, carry):
        rows = pl.ds(pl.multiple_of(rh * 8, 8), 8)
        slab = x_ref[:, rows, :].reshape(8 * n_a, tc)
        if prec is None:
            slab = slab.astype(BF16)
        out = jnp.dot(s1, slab, precision=prec, preferred_element_type=F32)
        ar_ref[:, rows, :] = out[: 8 * nk].reshape(nk, 8, tc)
        ai_ref[:, rows, :] = out[8 * nk:].reshape(nk, 8, tc)
        return carry

    lax.fori_loop(0, HY_R // 8, body, 0)


def _hy_cmul_tw(xr, xi, twr, twi, conj):
    reps = xr.shape[-1] // LANES
    if reps > 1:
        twr = jnp.concatenate([twr] * reps, axis=1)
        twi = jnp.concatenate([twi] * reps, axis=1)
    if conj:
        return xr * twr + xi * twi, xi * twr - xr * twi
    return xr * twr - xi * twi, xi * twr + xr * twi


def _hy_dft256(fs, xr, xi, inverse, prec):
    if prec is None:
        xr, xi = xr.astype(BF16), xi.astype(BF16)
    t1 = jnp.dot(fs, xr, precision=prec, preferred_element_type=F32)
    t2 = jnp.dot(fs, xi, precision=prec, preferred_element_type=F32)
    if inverse:
        return t1[:HY_R] + t2[HY_R:], t2[:HY_R] - t1[HY_R:]
    return t1[:HY_R] - t2[HY_R:], t2[:HY_R] + t1[HY_R:]


def _hy_filter_kernel(w1t_ref, w1c_ref, w1s_ref, b1_ref, w2_ref, b2_ref, w3_ref, freq_ref, dl_ref, o_ref, norm_ref, *, L):
    i = pl.program_id(0)
    tn = o_ref.shape[0]
    n = i * tn + lax.broadcasted_iota(jnp.int32, (tn, 1), 0)
    t = jnp.where(n < L, n, 2 * L - n).astype(F32)
    t_unit = t / max(L - 1, 1)
    j = lax.broadcasted_iota(jnp.int32, (1, HY_BANDS), 1).astype(F32)
    bands = 1e-4 + j * ((HY_BANDS - 1 - 1e-4) / (HY_BANDS - 1))
    ang = (2.0 * math.pi / L) * t * bands
    f = freq_ref[...]
    dot = functools.partial(jnp.dot, precision=HIGHEST, preferred_element_type=F32)
    h = t_unit * w1t_ref[...] + dot(jnp.cos(ang), w1c_ref[...]) - dot(jnp.sin(ang), w1s_ref[...])
    h = jnp.sin(f * (h + b1_ref[...]))
    h = jnp.sin(f * (dot(h, w2_ref[...]) + b2_ref[...]))
    hh = dot(h, w3_ref[...])
    half = 2 * HY_W
    k = jnp.where(n < L, hh[:, :half], hh[:, half:]) * jnp.exp(-t_unit * dl_ref[...])
    k = jnp.where(n == L, 0.0, k)
    o_ref[...] = k

    @pl.when(i == 0)
    def _():
        norm_ref[...] = jnp.zeros_like(norm_ref)

    norm_ref[...] += jnp.sum(jnp.abs(k), axis=0, keepdims=True)


def hyena_filter_time(L, w1, b1, w2, b2, w3, freq):
    tn = min(512, 2 * L)
    deltas = np.abs(np.linspace(HY_MIN_DECAY, HY_MAX_DECAY, HY_W, dtype=np.float32))
    dl = jnp.asarray(np.tile(deltas, 2)[None, :], F32)
    const = lambda shape: pl.BlockSpec(shape, lambda i: (0, 0))
    return pl.pallas_call(
        functools.partial(_hy_filter_kernel, L=L),
        grid=(2 * L // tn,),
        in_specs=[const((1, HY_HID)), const((HY_BANDS, HY_HID)), const((HY_BANDS, HY_HID)), const((1, HY_HID)),
                  const((HY_HID, HY_HID)), const((1, HY_HID)), const((HY_HID, 4 * HY_W)), const((1, HY_HID)),
                  const((1, 2 * HY_W))],
        out_specs=(pl.BlockSpec((tn, 2 * HY_W), lambda i: (i, 0)), const((1, 2 * HY_W))),
        out_shape=(jax.ShapeDtypeStruct((2 * L, 2 * HY_W), F32), jax.ShapeDtypeStruct((1, 2 * HY_W), F32)),
        compiler_params=_params("arbitrary"),
        name="hyena_filter_time",
    )(w1[0:1], w1[1:1 + HY_BANDS], w1[1 + HY_BANDS:], b1.reshape(1, -1), w2, b2.reshape(1, -1), w3,
      freq.reshape(1, -1), dl)


def _hy_spectrum_kernel(k_ref, norm_ref, s1_ref, fs_ref, twr_ref, twi_ref, kr_ref, ki_ref, ar_s, ai_s, *, L):
    ka = pl.program_id(1)
    na = _hy_na(L)
    nk = na // 2 + 1

    @pl.when(ka == 0)
    def _():
        _hy_stage1(k_ref, s1_ref[...], ar_s, ai_s, na, nk, HIGHEST)

    xr, xi = _hy_cmul_tw(ar_s[ka], ai_s[ka], twr_ref[...], twi_ref[...], False)
    br, bi = _hy_dft256(fs_ref[...], xr, xi, False, HIGHEST)
    scale = 1.0 / (2.0 * L * norm_ref[...])
    kr_ref[...] = br * scale
    ki_ref[...] = bi * scale


def hyena_filter_spectrum(L, k_time, norm, tabs):
    na = _hy_na(L)
    nk = na // 2 + 1
    tc = HY_TC
    out = jax.ShapeDtypeStruct((nk, HY_R, 2 * HY_W), F32)
    return pl.pallas_call(
        functools.partial(_hy_spectrum_kernel, L=L),
        grid=(2 * HY_W // tc, nk),
        in_specs=[pl.BlockSpec((na, HY_R, tc), lambda j, ka: (0, 0, j)),
                  pl.BlockSpec((1, tc), lambda j, ka: (0, j)),
                  pl.BlockSpec((16 * nk, 8 * na), lambda j, ka: (0, 0)),
                  pl.BlockSpec((2 * HY_R, HY_R), lambda j, ka: (0, 0)),
                  pl.BlockSpec((None, HY_R, LANES), lambda j, ka: (ka, 0, 0)),
                  pl.BlockSpec((None, HY_R, LANES), lambda j, ka: (ka, 0, 0))],
        out_specs=(pl.BlockSpec((None, HY_R, tc), lambda j, ka: (ka, 0, j)),
                   pl.BlockSpec((None, HY_R, tc), lambda j, ka: (ka, 0, j))),
        out_shape=(out, out),
        scratch_shapes=[pltpu.VMEM((nk, HY_R, tc), F32), pltpu.VMEM((nk, HY_R, tc), F32)],
        compiler_params=_params("arbitrary", "arbitrary"),
        name="hyena_filter_spectrum",
    )(k_time.reshape(na, HY_R, 2 * HY_W), norm, tabs["s1_full"], tabs["fs"], tabs["tw_r"], tabs["tw_i"])


def _hy_conv_kernel(v_ref, x1_ref, x2_ref, cw_ref, cb_ref, skip_ref, s1_ref, fs_ref, g_ref, twr_ref, twi_ref,
                    kr_ref, ki_ref, o_ref, xin_s, ar_s, ai_s, *, L):
    s = pl.program_id(2)
    na = _hy_na(L)
    nh = na // 2
    nk = nh + 1
    tc = HY_TC
    ka = s % nk

    def short_conv(ref, part):
        row = lax.broadcasted_iota(jnp.int32, (L, 1), 0)
        u = ref[...].astype(F32)
        up = jnp.where(row == 0, 0.0, pltpu.roll(u, 1, 0))
        dn = jnp.where(row == L - 1, 0.0, pltpu.roll(u, L - 1, 0))
        w = cw_ref[part]
        y = w[0:1, :] * up + w[1:2, :] * u + w[2:3, :] * dn + cb_ref[part]
        return y.reshape(nh, HY_R, tc)

    @pl.when(s == 0)
    def _():
        xin_s[...] = short_conv(v_ref, 0)

    @pl.when(ka == 0)
    def _():
        _hy_stage1(xin_s, s1_ref[...].astype(BF16), ar_s, ai_s, nh, nk, None)

    fs = fs_ref[...].astype(BF16)
    twr, twi = twr_ref[...], twi_ref[...]
    xr, xi = _hy_cmul_tw(ar_s[ka], ai_s[ka], twr, twi, False)
    br, bi = _hy_dft256(fs, xr, xi, False, None)
    kr, ki = kr_ref[...], ki_ref[...]
    pr, pi_ = br * kr - bi * ki, br * ki + bi * kr
    qr, qi = _hy_dft256(fs, pr, pi_, True, None)
    qr, qi = _hy_cmul_tw(qr, qi, twr, twi, True)
    ar_s[ka] = qr
    ai_s[ka] = qi

    def finish(order):
        gm = g_ref[...].astype(BF16)
        sk = skip_ref[order]

        def body(rh, carry):
            rows = pl.ds(pl.multiple_of(rh * 8, 8), 8)
            q = jnp.concatenate([ar_s[:, rows, :].reshape(8 * nk, tc), ai_s[:, rows, :].reshape(8 * nk, tc)], axis=0)
            y = jnp.dot(gm, q.astype(BF16), preferred_element_type=F32).reshape(nh, 8, tc)
            xin_s[:, rows, :] = y + xin_s[:, rows, :] * sk
            return carry

        lax.fori_loop(0, HY_R // 8, body, 0)

    @pl.when(s == nk - 1)
    def _():
        finish(0)
        xin_s[...] = short_conv(x1_ref, 1) * xin_s[...]

    @pl.when(s == 2 * nk - 1)
    def _():
        finish(1)
        o_ref[...] = (short_conv(x2_ref, 2) * xin_s[...]).reshape(L, tc).astype(o_ref.dtype)


def hyena_conv(p_hy, row0, L, conv_w, conv_b, skip, kf_r, kf_i, tabs):
    na = _hy_na(L)
    nh, nk = na // 2, na // 2 + 1
    tc = HY_TC
    nct = HY_W // tc
    rb0 = row0 // L
    cw = conv_w.reshape(3, 3, HY_W).transpose(1, 0, 2)
    cb = conv_b.reshape(3, 1, HY_W)
    col = lambda part: pl.BlockSpec((L, tc), lambda b, c, s, part=part: (rb0 + b, part * nct + c))
    const2 = lambda shape: pl.BlockSpec(shape, lambda b, c, s: (0, 0))
    tw_spec = pl.BlockSpec((None, HY_R, LANES), lambda b, c, s: (s % nk, 0, 0))
    kf_spec = pl.BlockSpec((None, HY_R, tc), lambda b, c, s: (s % nk, 0, (s // nk) * nct + c))
    return pl.pallas_call(
        functools.partial(_hy_conv_kernel, L=L),
        grid=(B, nct, 2 * nk),
        in_specs=[col(0), col(1), col(2),
                  pl.BlockSpec((3, 3, tc), lambda b, c, s: (0, 0, c)),
                  pl.BlockSpec((3, 1, tc), lambda b, c, s: (0, 0, c)),
                  pl.BlockSpec((2, 1, tc), lambda b, c, s: (0, 0, c)),
                  const2((16 * nk, 8 * nh)), const2((2 * HY_R, HY_R)), const2((8 * nh, 16 * nk)),
                  tw_spec, tw_spec, kf_spec, kf_spec],
        out_specs=pl.BlockSpec((L, tc), lambda b, c, s: (b, c)),
        out_shape=jax.ShapeDtypeStruct((B * L, HY_W), BF16),
        scratch_shapes=[pltpu.VMEM((nh, HY_R, tc), F32), pltpu.VMEM((nk, HY_R, tc), F32),
                        pltpu.VMEM((nk, HY_R, tc), F32)],
        compiler_params=_params("arbitrary", "arbitrary", "arbitrary"),
        name="hyena_conv_%d" % L,
    )(p_hy, p_hy, p_hy, cw, cb, skip.reshape(2, 1, HY_W), tabs["s1_half"], tabs["fs"], tabs["g"],
      tabs["tw_r"], tabs["tw_i"], kf_r, kf_i)


def hyena_branch(p_hy, row0, L, conv_w, conv_b, w1, b1, w2, b2, w3, freq, skip):
    tabs = _hy_tables(L)
    k_time, norm = hyena_filter_time(L, w1, b1, w2, b2, w3, freq)
    kf_r, kf_i = hyena_filter_spectrum(L, k_time, norm, tabs)
    return hyena_conv(p_hy, row0, L, conv_w, conv_b, skip, kf_r, kf_i, tabs)


def kernel(x, c, ctx, c_ctx, w_mod, b_mod, w_in, hy_conv_w, hy_conv_b, hy_w1, hy_b1, hy_w2, hy_b2, hy_w3, hy_freq, hy_skip, ssd_conv_w, ssd_conv_b, ssd_a_log, ssd_dt_bias, ssd_d, ssd_norm_g, da_lambda, da_subln_g, w_out, ln1_g, ln1_b, router, w_gate, w_up, w_down, ln2_g, ln2_b):
    xa = jnp.concatenate([x.reshape(NL, D), ctx.reshape(NC, D)], axis=0)
    cc = jnp.concatenate([c, c_ctx[None, :], jnp.zeros((8 - B - 1, D), F32)], axis=0)
    mod = modulation(cc, w_mod, b_mod).reshape(DEPTH, 8, 1, 6 * D)
    rope = rope_tables()
    o_ssd = HY_PROJ
    o_xbc = o_ssd + SSD_W
    o_dt = o_xbc + SSD_XBC
    o_da = o_dt + 2 * SSD_H
    w_hy = w_in[:, :, :o_ssd].astype(BF16)
    w_ssd = jnp.concatenate([w_in[:, :, o_xbc:o_dt], w_in[:, :, o_ssd:o_xbc], w_in[:, :, o_dt:o_da],
                             jnp.zeros((DEPTH, D, LANES - 2 * SSD_H), F32)], axis=2).astype(BF16)
    w_da = w_in[:, :, o_da:].astype(BF16)
    w_out_b = w_out.astype(BF16)
    r_pad = jnp.pad(router, ((0, 0), (0, 0), (0, LANES - N_EXPERTS))).astype(BF16)
    for l in range(DEPTH):
        lam_init = 0.8 - 0.6 * math.exp(-0.3 * l)
        p_hy = in_projection(xa, mod[l], w_hy, l, "hy")
        zx, dt = in_projection(xa, mod[l], w_ssd, l, "ssd")
        da = in_projection(xa, mod[l], w_da, l, "da", rope)

        hy_args = (hy_conv_w[l], hy_conv_b[l], hy_w1[l], hy_b1[l], hy_w2[l], hy_b2[l], hy_w3[l], hy_freq[l], hy_skip[l])
        y_hy = jnp.concatenate([hyena_branch(p_hy, 0, SEQ, *hy_args), hyena_branch(p_hy, NL, CTX, *hy_args)], axis=0)

        yf, yb = ssd_scan(zx, dt, ssd_conv_w[l], ssd_conv_b[l], ssd_a_log[l], ssd_dt_bias[l], ssd_d[l])
        y_ssd = ssd_gate(yf, yb, zx, ssd_norm_g[l])

        lp = da_lambda[l].astype(F32)
        lam = jnp.exp(jnp.sum(lp[0] * lp[1])) - jnp.exp(jnp.sum(lp[2] * lp[3])) + lam_init
        y_da = jnp.concatenate([diff_attention(da, lam, da_subln_g[l], lam_init, False),
                                diff_attention(da, lam, da_subln_g[l], lam_init, True)], axis=0)

        x1, h2, aff = out_projection(y_hy, y_ssd, y_da, w_out_b, l, xa, mod[l], ln1_g[l], ln1_b[l], r_pad)
        idx, tw, pos, starts, rounds = moe_route(aff)
        y = expert_ffn(idx, h2, tw, w_gate, w_up, w_down, l)
        xa = moe_combine(starts, rounds, pos, y, x1, mod[l], ln2_g[l], ln2_b[l])
    return xa[:NL].reshape(B, SEQ, D)
```

```python
import functools
import math

import numpy as np
import jax
import jax.numpy as jnp
from jax import lax
from jax.experimental import pallas as pl
from jax.experimental.pallas import tpu as pltpu

F32 = jnp.float32
BF16 = jnp.bfloat16
HIGHEST = lax.Precision.HIGHEST

D = 2048
B = 4
SEQ = 4096
DEPTH = 4
GRID_W = 64
CTX = 256
HY_W = 512
SSD_W = 512
DA_W = 1024
HY_EMB = 33
HY_BANDS = 16
HY_HID = 64
HY_MIN_DECAY = math.log(1e-2) / 1.5
HY_MAX_DECAY = math.log(1e-2) / 0.3
HY_PROJ = 3 * HY_W
SSD_H = 8
SSD_N = 128
SSD_CHUNK = 128
SSD_XBC = 1024
DA_H = 8
DA_DK = 64
DA_QK = 1024
ROPE_THETA = 10000.0
ROPE_PAIRS = 16
N_EXPERTS = 16
EC_FACTOR = 2
D_FF = 1536
ALPHA = (2 * DEPTH) ** 0.25
LN_EPS = 1e-6

NL = B * SEQ
NC = B * CTX
M = NL + NC
LANES = 128
TM = 256
VMEM_LIMIT = 56 * 1024 * 1024

N_LAT_TILES = NL // TM
TILES_PER_SAMPLE = SEQ // TM


def _mod_row(i):
    return jnp.where(i < N_LAT_TILES, i // TILES_PER_SAMPLE, B)


def _mod_spec(chunk):
    return pl.BlockSpec((None, 1, D), lambda i, chunk=chunk: (_mod_row(i), 0, chunk))


def _params(*sem):
    return pltpu.CompilerParams(dimension_semantics=sem, vmem_limit_bytes=VMEM_LIMIT)


def _ln(x):
    mu = jnp.mean(x, axis=-1, keepdims=True)
    xc = x - mu
    var = jnp.mean(xc * xc, axis=-1, keepdims=True)
    return xc * lax.rsqrt(var + LN_EPS)


def _silu(x):
    return x * jax.nn.sigmoid(x)


def _mod_kernel(c_ref, w_ref, b_ref, o_ref):
    s = _silu(c_ref[...])
    o_ref[...] = jnp.dot(s, w_ref[...], precision=HIGHEST, preferred_element_type=F32) + b_ref[...]


def modulation(cc, w_mod, b_mod):
    tn = 1024
    return pl.pallas_call(
        _mod_kernel,
        grid=(DEPTH, 6 * D // tn),
        in_specs=[
            pl.BlockSpec((8, D), lambda l, j: (0, 0)),
            pl.BlockSpec((None, D, tn), lambda l, j: (l, 0, j)),
            pl.BlockSpec((None, 1, tn), lambda l, j: (l, 0, j)),
        ],
        out_specs=pl.BlockSpec((None, 8, tn), lambda l, j: (l, 0, j)),
        out_shape=jax.ShapeDtypeStruct((DEPTH, 8, 6 * D), F32),
        compiler_params=_params("arbitrary", "arbitrary"),
        name="modulation",
    )(cc, w_mod, b_mod.reshape(DEPTH, 1, 6 * D))


def _inproj_kernel(x_ref, sh_ref, sc_ref, w_ref, *rest, mode):
    h = _ln(x_ref[...]) * (1.0 + sc_ref[...]) + sh_ref[...]
    acc = jnp.dot(h.astype(BF16), w_ref[...], preferred_element_type=F32)
    if mode == "hy":
        (o_ref,) = rest
        o_ref[...] = acc.astype(o_ref.dtype)
    elif mode == "ssd":
        o_ref, dt_ref = rest
        o_ref[...] = acc[:, : SSD_XBC + SSD_W].astype(o_ref.dtype)
        dt_ref[...] = acc[:, SSD_XBC + SSD_W:]
    else:
        cos_ref, sin_ref, o_ref = rest
        cs = cos_ref[...]
        sn = sin_ref[...]
        lane = lax.broadcasted_iota(jnp.int32, (TM, LANES), 1)
        is_a = (lane % 32) < ROPE_PAIRS
        for g in range(2 * DA_QK // LANES):
            blk = acc[:, g * LANES:(g + 1) * LANES]
            swapped = jnp.where(is_a, pltpu.roll(blk, LANES - ROPE_PAIRS, 1), pltpu.roll(blk, ROPE_PAIRS, 1))
            rot = blk * cs + swapped * sn
            if g < DA_QK // LANES:
                rot = rot * (DA_DK ** -0.5 * math.log2(math.e))
            o_ref[:, g * LANES:(g + 1) * LANES] = rot.astype(o_ref.dtype)
        o_ref[:, 2 * DA_QK:] = acc[:, 2 * DA_QK:].astype(o_ref.dtype)


def in_projection(x, mod_l, w, l, mode, rope=None):
    n = w.shape[2]
    in_specs = [
        pl.BlockSpec((TM, D), lambda i: (i, 0)),
        _mod_spec(0),
        _mod_spec(1),
        pl.BlockSpec((None, D, n), lambda i: (l, 0, 0)),
    ]
    args = [x, mod_l, mod_l, w]
    if mode == "hy":
        out_shape = jax.ShapeDtypeStruct((M, n), BF16)
        out_specs = pl.BlockSpec((TM, n), lambda i: (i, 0))
    elif mode == "ssd":
        n0 = SSD_XBC + SSD_W
        out_shape = (jax.ShapeDtypeStruct((M, n0), BF16), jax.ShapeDtypeStruct((M, LANES), F32))
        out_specs = (pl.BlockSpec((TM, n0), lambda i: (i, 0)), pl.BlockSpec((TM, LANES), lambda i: (i, 0)))
    else:
        rope_idx = lambda i: (jnp.where(i < N_LAT_TILES, i % TILES_PER_SAMPLE, TILES_PER_SAMPLE), 0)
        in_specs += [pl.BlockSpec((TM, LANES), rope_idx), pl.BlockSpec((TM, LANES), rope_idx)]
        args += list(rope)
        out_shape = jax.ShapeDtypeStruct((M, n), BF16)
        out_specs = pl.BlockSpec((TM, n), lambda i: (i, 0))
    return pl.pallas_call(
        functools.partial(_inproj_kernel, mode=mode),
        grid=(M // TM,),
        in_specs=in_specs,
        out_specs=out_specs,
        out_shape=out_shape,
        compiler_params=_params("arbitrary"),
        name="inproj_" + mode,
    )(*args)


def rope_tables():
    pos = np.arange(SEQ)
    r = (pos // GRID_W).astype(np.float64)
    col = (pos % GRID_W).astype(np.float64)
    inv = ROPE_THETA ** (-np.arange(ROPE_PAIRS, dtype=np.float64) / ROPE_PAIRS)
    inv = inv.astype(np.float32).astype(np.float64)
    lane = np.arange(LANES)
    d = lane % DA_DK
    axis = d // 32
    pair = d % ROPE_PAIRS
    is_b = (d % 32) >= ROPE_PAIRS
    ang = np.where(axis[None, :] == 0, r[:, None], col[:, None]) * inv[pair][None, :]
    cos = np.cos(ang)
    sin = np.where(is_b[None, :], np.sin(ang), -np.sin(ang))
    cos = np.concatenate([cos, np.ones((TM, LANES))], axis=0)
    sin = np.concatenate([sin, np.zeros((TM, LANES))], axis=0)
    return jnp.asarray(cos, F32), jnp.asarray(sin, F32)


ATT_SUB = 256


def _attn_kernel(lam_ref, q_ref, *rest, n_seg, post_scale):
    kv = rest[: 2 * n_seg]
    g_ref, o_ref = rest[2 * n_seg:]
    lam = lam_ref[0]
    nt = (((1,), (1,)), ((), ()))
    for sub in range(q_ref.shape[0] // ATT_SUB):
        rows = slice(sub * ATT_SUB, (sub + 1) * ATT_SUB)
        q = q_ref[rows, :]
        lane = lax.broadcasted_iota(jnp.int32, q.shape, 1)
        zero = jnp.zeros_like(q)
        qm = (jnp.where(lane < DA_DK, q, zero), jnp.where(lane >= DA_DK, q, zero))
        scores = [[lax.dot_general(qm[m], kv[2 * s][...], nt, preferred_element_type=F32) for s in range(n_seg)]
                  for m in range(2)]
        probs = []
        for m in range(2):
            mx = functools.reduce(jnp.maximum, [jnp.max(s, axis=-1, keepdims=True) for s in scores[m]])
            es = [jnp.exp2(s - mx) for s in scores[m]]
            tot = functools.reduce(lambda a, b: a + b, [jnp.sum(e, axis=-1, keepdims=True) for e in es])
            probs.append((es, 1.0 / tot))
        out = None
        for s in range(n_seg):
            a = probs[0][0][s] * probs[0][1] - (lam * probs[1][1]) * probs[1][0][s]
            o = jnp.dot(a.astype(BF16), kv[2 * s + 1][...], preferred_element_type=F32)
            out = o if out is None else out + o
        out = out * lax.rsqrt(jnp.mean(out * out, axis=-1, keepdims=True) + LN_EPS)
        o_ref[rows, :] = ((out * g_ref[...]) * post_scale).astype(o_ref.dtype)


def diff_attention(da, lam, subln_g, lam_init, ctx_queries):
    tq = CTX if ctx_queries else 2 * ATT_SUB
    lam = jnp.reshape(lam, (1,)).astype(F32)
    g = subln_g.reshape(1, LANES)
    kc_spec = pl.BlockSpec((CTX, LANES), lambda b, h, i: (NL // CTX + b, DA_H + h))
    vc_spec = pl.BlockSpec((CTX, LANES), lambda b, h, i: (NL // CTX + b, 2 * DA_H + h))
    if ctx_queries:
        rows, nq = NC, CTX // tq
        q_spec = pl.BlockSpec((tq, LANES), lambda b, h, i: ((NL + b * CTX) // tq + i, h))
        segs, seg_args = [kc_spec, vc_spec], [da, da]
    else:
        rows, nq = NL, SEQ // tq
        q_spec = pl.BlockSpec((tq, LANES), lambda b, h, i: (b * nq + i, h))
        kl_spec = pl.BlockSpec((SEQ, LANES), lambda b, h, i: (b, DA_H + h))
        vl_spec = pl.BlockSpec((SEQ, LANES), lambda b, h, i: (b, 2 * DA_H + h))
        segs, seg_args = [kl_spec, vl_spec, kc_spec, vc_spec], [da, da, da, da]
    return pl.pallas_call(
        functools.partial(_attn_kernel, n_seg=len(segs) // 2, post_scale=1.0 - lam_init),
        grid=(B, DA_H, nq),
        in_specs=[pl.BlockSpec(memory_space=pltpu.SMEM), q_spec] + segs + [pl.BlockSpec((1, LANES), lambda b, h, i: (0, 0))],
        out_specs=pl.BlockSpec((tq, LANES), lambda b, h, i: (b * nq + i, h)),
        out_shape=jax.ShapeDtypeStruct((rows, DA_W), BF16),
        compiler_params=_params("arbitrary", "arbitrary", "arbitrary"),
        name="diff_attn_ctx" if ctx_queries else "diff_attn_lat",
    )(lam, da, *seg_args, g)


SSD_NC_CTX = CTX // SSD_CHUNK
SSD_NC_LAT = SEQ // SSD_CHUNK
SSD_STEPS = SSD_NC_CTX + SSD_NC_LAT


def _ssd_chunk(s, direction):
    in_ctx = s < SSD_NC_CTX
    k = jnp.where(in_ctx, s, s - SSD_NC_CTX)
    nck = jnp.where(in_ctx, SSD_NC_CTX, SSD_NC_LAT)
    if direction == 1:
        k = nck - 1 - k
    return in_ctx, k, nck


def _ssd_rowblock(b, s, direction):
    in_ctx, k, _ = _ssd_chunk(s, direction)
    return jnp.where(in_ctx, NL // SSD_CHUNK + b * SSD_NC_CTX + k, b * SSD_NC_LAT + k)


def _ssd_kernel(xf_ref, xfp_ref, xfn_ref, dtf_ref, xb_ref, xbp_ref, xbn_ref, dtb_ref,
                cw_ref, cb_ref, a_ref, dtbias_ref, dskip_ref, yf_ref, yb_ref, state_ref):
    s = pl.program_id(1)
    q = SSD_CHUNK

    @pl.when(s == 0)
    def _():
        state_ref[...] = jnp.zeros_like(state_ref)

    row = lax.broadcasted_iota(jnp.int32, (q, 1), 0)
    ri = lax.broadcasted_iota(jnp.int32, (q, q), 0)
    ci = lax.broadcasted_iota(jnp.int32, (q, q), 1)
    lane_lo = lax.broadcasted_iota(jnp.int32, (1, LANES), 1) < 64
    cw = cw_ref[...]
    dirs = ((xf_ref, xfp_ref, xfn_ref, dtf_ref, yf_ref), (xb_ref, xbp_ref, xbn_ref, dtb_ref, yb_ref))
    for d, (x_ref, xp_ref, xn_ref, dt_ref, y_ref) in enumerate(dirs):
        _, k, nck = _ssd_chunk(s, d)
        u = x_ref[...].astype(F32)
        prev = jnp.where(k == 0, 0.0, xp_ref[15:16, :].astype(F32))
        nxt = jnp.where(k == nck - 1, 0.0, xn_ref[0:1, :].astype(F32))
        up = jnp.where(row == 0, prev, pltpu.roll(u, 1, 0))
        dn = jnp.where(row == q - 1, nxt, pltpu.roll(u, q - 1, 0))
        conv = cw[0:1, :] * up + cw[1:2, :] * u + cw[2:3, :] * dn + cb_ref[...]
        xbc = _silu(conv)
        xs = xbc[:, :SSD_W]
        bm = xbc[:, SSD_W:SSD_W + 2 * SSD_N]
        cm = xbc[:, SSD_W + 2 * SSD_N:]
        z = dt_ref[...] + dtbias_ref[...]
        dt = jnp.maximum(z, 0.0) + jnp.log(1.0 + jnp.exp(-jnp.abs(z)))
        a = dt * a_ref[...]
        tri = (ci <= ri) if d == 0 else (ci >= ri)
        a_cum = jnp.dot(tri.astype(F32), a, precision=HIGHEST, preferred_element_type=F32)
        a_cum_t = a_cum.T
        ea = jnp.exp(a_cum)
        a_tot = a_cum[q - 1:q, :] if d == 0 else a_cum[0:1, :]
        to_end = jnp.exp(a_tot - a_cum)
        chunk_decay = jnp.exp(a_tot)

        def halves(col_arr, c0, c1):
            return jnp.where(lane_lo, col_arr[:, c0:c0 + 1], col_arr[:, c1:c1 + 1])

        for g in range(2):
            bg = bm[:, g * SSD_N:(g + 1) * SSD_N]
            cgb = cm[:, g * SSD_N:(g + 1) * SSD_N].astype(BF16)
            gmat = lax.dot_general(cgb, bg.astype(BF16), (((1,), (1,)), ((), ())), preferred_element_type=F32)
            bgt = bg.T.astype(BF16)
            for pr in range(2):
                p = 2 * g + pr
                c0, c1 = 8 * d + 2 * p, 8 * d + 2 * p + 1
                xpair = xs[:, p * LANES:(p + 1) * LANES]
                xdt = xpair * halves(dt, c0, c1)
                y = None
                for hh, hc in ((0, c0), (1, c1)):
                    seg = a_cum[:, hc:hc + 1] - a_cum_t[hc:hc + 1, :]
                    lmat = jnp.exp(jnp.where(tri, seg, -jnp.inf))
                    xm = jnp.where(lane_lo if hh == 0 else jnp.logical_not(lane_lo), xdt, 0.0)
                    t = jnp.dot((gmat * lmat).astype(BF16), xm.astype(BF16), preferred_element_type=F32)
                    y = t if y is None else y + t
                st = state_ref[d, p]
                y = y + halves(ea, c0, c1) * jnp.dot(cgb, st.astype(BF16), preferred_element_type=F32)
                upd = jnp.dot(bgt, (xdt * halves(to_end, c0, c1)).astype(BF16), preferred_element_type=F32)
                state_ref[d, p] = halves(chunk_decay, c0, c1) * st + upd
                if d == 0:
                    y = y + dskip_ref[:, p * LANES:(p + 1) * LANES] * xpair
                y_ref[:, p * LANES:(p + 1) * LANES] = y


def ssd_scan(zx, dt, conv_w, conv_b, a_log, dt_bias, d_skip):
    n16 = M // 16

    def chunk_spec(direction):
        return pl.BlockSpec((SSD_CHUNK, SSD_XBC), lambda b, s: (_ssd_rowblock(b, s, direction), 0))

    def prev_spec(direction):
        return pl.BlockSpec((16, SSD_XBC), lambda b, s: (jnp.maximum(_ssd_rowblock(b, s, direction) * 8 - 1, 0), 0))

    def next_spec(direction):
        return pl.BlockSpec((16, SSD_XBC), lambda b, s: (jnp.minimum(_ssd_rowblock(b, s, direction) * 8 + 8, n16 - 1), 0))

    def dt_spec(direction):
        return pl.BlockSpec((SSD_CHUNK, LANES), lambda b, s: (_ssd_rowblock(b, s, direction), 0))

    def y_spec(direction):
        return pl.BlockSpec((SSD_CHUNK, SSD_W), lambda b, s: (_ssd_rowblock(b, s, direction), 0))

    const = lambda shape: pl.BlockSpec(shape, lambda b, s: (0, 0))
    pad16 = lambda v: jnp.pad(v.reshape(1, 2 * SSD_H).astype(F32), ((0, 0), (0, LANES - 2 * SSD_H)))
    a_row = pad16(-jnp.exp(a_log.astype(F32)))
    bias_row = pad16(dt_bias)
    dskip_row = jnp.repeat(d_skip.astype(F32), SSD_W // SSD_H).reshape(1, SSD_W)
    return pl.pallas_call(
        _ssd_kernel,
        grid=(B, SSD_STEPS),
        in_specs=[chunk_spec(0), prev_spec(0), next_spec(0), dt_spec(0),
                  chunk_spec(1), prev_spec(1), next_spec(1), dt_spec(1),
                  const((3, SSD_XBC)), const((1, SSD_XBC)), const((1, LANES)), const((1, LANES)), const((1, SSD_W))],
        out_specs=(y_spec(0), y_spec(1)),
        out_shape=(jax.ShapeDtypeStruct((M, SSD_W), F32), jax.ShapeDtypeStruct((M, SSD_W), F32)),
        scratch_shapes=[pltpu.VMEM((2, 4, SSD_N, LANES), F32)],
        compiler_params=_params("arbitrary", "arbitrary"),
        name="ssd_scan",
    )(zx, zx, zx, dt, zx, zx, zx, dt, conv_w, conv_b.reshape(1, SSD_XBC), a_row, bias_row, dskip_row)


def _ssd_gate_kernel(yf_ref, yb_ref, z_ref, g_ref, o_ref):
    y = (yf_ref[...] + yb_ref[...]) * _silu(z_ref[...].astype(F32))
    y = y * lax.rsqrt(jnp.mean(y * y, axis=-1, keepdims=True) + LN_EPS)
    o_ref[...] = (y * g_ref[...]).astype(o_ref.dtype)


def ssd_gate(yf, yb, zx, norm_g):
    tm = 512
    row = lambda i: (i, 0)
    return pl.pallas_call(
        _ssd_gate_kernel,
        grid=(M // tm,),
        in_specs=[pl.BlockSpec((tm, SSD_W), row), pl.BlockSpec((tm, SSD_W), row),
                  pl.BlockSpec((tm, SSD_W), lambda i: (i, SSD_XBC // SSD_W)),
                  pl.BlockSpec((1, SSD_W), lambda i: (0, 0))],
        out_specs=pl.BlockSpec((tm, SSD_W), row),
        out_shape=jax.ShapeDtypeStruct((M, SSD_W), BF16),
        compiler_params=_params("arbitrary"),
        name="ssd_gate",
    )(yf, yb, zx, norm_g.reshape(1, SSD_W))


def _pack_halves(v):
    n = v.shape[-1] // 2
    lo = lax.shift_right_logical(lax.bitcast_convert_type(v[:, :n], jnp.int32), 16)
    hi = lax.bitcast_convert_type(v[:, n:], jnp.int32) & jnp.int32(-65536)
    return hi | lo


def _unpack_halves(w):
    lo = lax.bitcast_convert_type(lax.shift_left(w, 16), F32).astype(BF16)
    hi = lax.bitcast_convert_type(w & jnp.int32(-65536), F32).astype(BF16)
    return lo, hi


def _outproj_kernel(yh_ref, ys_ref, ya_ref, w_ref, x_ref, g1_ref, lng_ref, lnb_ref, sh2_ref, sc2_ref, r_ref,
                    x1_ref, h2_ref, aff_ref):
    y = jnp.dot(yh_ref[...], w_ref[0:HY_W, :], preferred_element_type=F32)
    y = y + jnp.dot(ys_ref[...], w_ref[HY_W:HY_W + SSD_W, :], preferred_element_type=F32)
    y = y + jnp.dot(ya_ref[...], w_ref[HY_W + SSD_W:, :], preferred_element_type=F32)
    x1 = _ln(ALPHA * x_ref[...] + g1_ref[...] * y) * lng_ref[...] + lnb_ref[...]
    x1_ref[...] = x1
    h2 = (_ln(x1) * (1.0 + sc2_ref[...]) + sh2_ref[...]).astype(BF16)
    h2_ref[...] = _pack_halves(h2.astype(F32))
    logits = jnp.dot(h2, r_ref[...], preferred_element_type=F32)
    lane = lax.broadcasted_iota(jnp.int32, logits.shape, 1)
    logits = jnp.where(lane < N_EXPERTS, logits, -jnp.inf)
    e = jnp.exp(logits - jnp.max(logits, axis=-1, keepdims=True))
    aff_ref[...] = e / jnp.sum(e, axis=-1, keepdims=True)


def out_projection(yh, ys, ya, w_out, l, x, mod_l, ln_g, ln_b, r_pad):
    row = lambda i: (i, 0)
    const = lambda i: (0, 0)
    layer = lambda i: (l, 0, 0)
    return pl.pallas_call(
        _outproj_kernel,
        grid=(M // TM,),
        in_specs=[pl.BlockSpec((TM, HY_W), row), pl.BlockSpec((TM, SSD_W), row), pl.BlockSpec((TM, DA_W), row),
                  pl.BlockSpec((None, D, D), layer), pl.BlockSpec((TM, D), row), _mod_spec(2),
                  pl.BlockSpec((1, D), const), pl.BlockSpec((1, D), const), _mod_spec(3), _mod_spec(4),
                  pl.BlockSpec((None, D, LANES), layer)],
        out_specs=(pl.BlockSpec((TM, D), row), pl.BlockSpec((TM, D // 2), row), pl.BlockSpec((TM, LANES), row)),
        out_shape=(jax.ShapeDtypeStruct((M, D), F32), jax.ShapeDtypeStruct((M, D // 2), jnp.int32),
                   jax.ShapeDtypeStruct((M, LANES), F32)),
        compiler_params=_params("arbitrary"),
        name="outproj",
    )(yh, ys, ya, w_out, x, mod_l, ln_g.reshape(1, D), ln_b.reshape(1, D), mod_l, mod_l, r_pad)


CAP_LAT = EC_FACTOR * SEQ // N_EXPERTS
CAP_CTX = EC_FACTOR * CTX // N_EXPERTS
EROWS = B * (CAP_LAT + CAP_CTX)
FFN_TR = EROWS // 2
FFN_TF = 256
FFN_NF = D_FF // FFN_TF
FFN_STEPS = (EROWS // FFN_TR) * FFN_NF
FFN_GCHUNK = -(-EROWS // FFN_STEPS // 8) * 8
FFN_GROWS = FFN_STEPS * FFN_GCHUNK


def _gather_copy(h2_hbm, raw_s, sem, src_row, dst_row):
    return pltpu.make_async_copy(h2_hbm.at[pl.ds(src_row, 1), :], raw_s.at[pl.ds(dst_row, 1), :], sem)


def _ffn_kernel(idx_ref, idxn_ref, h2_hbm, tw_ref, wg_ref, wu_ref, wd_ref, o_ref, raw_s, xlo_s, xhi_s, acc_s, sem):
    e, r, f = pl.program_id(0), pl.program_id(1), pl.program_id(2)
    half = D // 2
    k = r * FFN_NF + f

    def wait_gather():
        def body(i, carry):
            _gather_copy(h2_hbm, raw_s, sem, 0, i).wait()
            return carry
        lax.fori_loop(0, FFN_GROWS, body, 0, unroll=16)

    @pl.when(k == 0)
    def _():
        @pl.when(e == 0)
        def _():
            def body(i, carry):
                _gather_copy(h2_hbm, raw_s, sem, idx_ref[0, i], i).start()
                return carry
            lax.fori_loop(0, FFN_GROWS, body, 0, unroll=8)

        wait_gather()
        lo, hi = _unpack_halves(raw_s[0:EROWS, :])
        xlo_s[...] = lo
        xhi_s[...] = hi

    base = k * FFN_GCHUNK
    for j in range(FFN_GCHUNK):
        _gather_copy(h2_hbm, raw_s, sem, idxn_ref[0, base + j], base + j).start()

    rows = pl.ds(pl.multiple_of(r * FFN_TR, 16), FFN_TR)
    xlo = xlo_s[rows, :]
    xhi = xhi_s[rows, :]

    def proj(w_ref):
        w = w_ref[...].astype(BF16)
        return (jnp.dot(xlo, w[:half], preferred_element_type=F32) + jnp.dot(xhi, w[half:], preferred_element_type=F32))

    g = proj(wg_ref)
    u = proj(wu_ref)
    t = jnp.dot((_silu(g) * u).astype(BF16), wd_ref[...].astype(BF16), preferred_element_type=F32)

    @pl.when(f == 0)
    def _():
        acc_s[...] = t

    @pl.when(f != 0)
    def _():
        acc_s[...] += t

    @pl.when(f == FFN_NF - 1)
    def _():
        y = (acc_s[...] * tw_ref[...]).astype(BF16).astype(F32)
        o_ref[...] = _pack_halves(y)

    @pl.when((e == N_EXPERTS - 1) & (k == FFN_STEPS - 1))
    def _():
        wait_gather()


def expert_ffn(idx, h2p, tw, w_gate, w_up, w_down, l):
    idx = jnp.pad(idx, ((0, 0), (0, 0), (0, FFN_GROWS - EROWS)))
    smem_idx = lambda shift: pl.BlockSpec((None, 1, FFN_GROWS), lambda e, r, f: (jnp.minimum(e + shift, N_EXPERTS - 1), 0, 0),
                                          memory_space=pltpu.SMEM)
    return pl.pallas_call(
        _ffn_kernel,
        grid=(N_EXPERTS, EROWS // FFN_TR, FFN_NF),
        in_specs=[smem_idx(0), smem_idx(1),
                  pl.BlockSpec(memory_space=pl.ANY),
                  pl.BlockSpec((None, FFN_TR, 1), lambda e, r, f: (e, r, 0)),
                  pl.BlockSpec((None, None, D, FFN_TF), lambda e, r, f: (l, e, 0, f)),
                  pl.BlockSpec((None, None, D, FFN_TF), lambda e, r, f: (l, e, 0, f)),
                  pl.BlockSpec((None, None, FFN_TF, D), lambda e, r, f: (l, e, f, 0))],
        out_specs=pl.BlockSpec((None, FFN_TR, D // 2), lambda e, r, f: (e, r, 0)),
        out_shape=jax.ShapeDtypeStruct((N_EXPERTS, EROWS, D // 2), jnp.int32),
        scratch_shapes=[pltpu.VMEM((FFN_GROWS, D // 2), jnp.int32), pltpu.VMEM((EROWS, D // 2), BF16),
                        pltpu.VMEM((EROWS, D // 2), BF16), pltpu.VMEM((FFN_TR, D), F32),
                        pltpu.SemaphoreType.DMA(())],
        compiler_params=_params("arbitrary", "arbitrary", "arbitrary"),
        name="expert_ffn",
    )(idx, idx, h2p, tw, w_gate, w_up, w_down)


SEL_BLK = 256


def _select_kernel(aff_ref, pos_ref, cex_ref, cin_ref, idx_ref, tw_ref, *, n, cap, row_base, slot_base):
    b = pl.program_id(0)
    aff = aff_ref[...]
    bits = lax.bitcast_convert_type(aff, jnp.int32)

    def search(i, thr):
        cand = thr | jnp.left_shift(jnp.int32(1), 30 - i)
        cnt = jnp.sum((bits >= cand).astype(jnp.int32), axis=0, keepdims=True)
        return jnp.where(cnt >= cap, cand, thr)

    thr = lax.fori_loop(0, 31, search, jnp.zeros((1, LANES), jnp.int32))
    gt = bits > thr
    eq = bits == thr
    need = (cap - jnp.sum(gt.astype(jnp.int32), axis=0, keepdims=True)).astype(F32)
    ri = lax.broadcasted_iota(jnp.int32, (SEL_BLK, SEL_BLK), 0)
    ci = lax.broadcasted_iota(jnp.int32, (SEL_BLK, SEL_BLK), 1)
    tri = (ci <= ri).astype(BF16)

    def prefix(mask):
        m = mask.astype(BF16)
        off = jnp.zeros((1, LANES), F32)
        parts = []
        for k in range(n // SEL_BLK):
            p = jnp.dot(tri, m[k * SEL_BLK:(k + 1) * SEL_BLK], preferred_element_type=F32) + off
            parts.append(p)
            off = p[SEL_BLK - 1:SEL_BLK, :]
        return jnp.concatenate(parts, axis=0) if len(parts) > 1 else parts[0]

    sel = gt | (eq & (prefix(eq) <= need))
    cnt = prefix(sel)
    base = slot_base + b * cap
    key = jnp.where(sel, cnt - 1.0, -1.0)
    pos_ref[...] = jnp.where(sel, cnt.astype(jnp.int32) - 1 + base, -1)
    cin_ref[...] = cnt.astype(jnp.int32) + base
    cex_ref[...] = cnt.astype(jnp.int32) - sel.astype(jnp.int32) + base
    slots = lax.broadcasted_iota(jnp.int32, (1, cap), 1).astype(F32)
    tok = (lax.broadcasted_iota(jnp.int32, (n, 1), 0) + (row_base + b * n)).astype(F32)
    for e in range(N_EXPERTS):
        hit = key[:, e:e + 1] == slots
        idx_ref[e:e + 1, :] = jnp.sum(jnp.where(hit, tok, 0.0), axis=0, keepdims=True).astype(jnp.int32)
        tw_ref[e:e + 1, :] = jnp.sum(jnp.where(hit, aff[:, e:e + 1], 0.0), axis=0, keepdims=True)


def moe_select(aff, n, cap, row_base, slot_base):
    rb = row_base // n
    tok = jax.ShapeDtypeStruct((B * n, LANES), jnp.int32)
    tok_spec = pl.BlockSpec((n, LANES), lambda b: (b, 0))
    return pl.pallas_call(
        functools.partial(_select_kernel, n=n, cap=cap, row_base=row_base, slot_base=slot_base),
        grid=(B,),
        in_specs=[pl.BlockSpec((n, LANES), lambda b: (rb + b, 0))],
        out_specs=(tok_spec, tok_spec, tok_spec,
                   pl.BlockSpec((None, N_EXPERTS, cap), lambda b: (b, 0, 0)),
                   pl.BlockSpec((None, N_EXPERTS, cap), lambda b: (b, 0, 0))),
        out_shape=(tok, tok, tok, jax.ShapeDtypeStruct((B, N_EXPERTS, cap), jnp.int32),
                   jax.ShapeDtypeStruct((B, N_EXPERTS, cap), F32)),
        compiler_params=_params("arbitrary"),
        name="moe_select_%d" % n,
    )(aff)


CMB_T = 128
CMB_J = 40
CMB_STEP = CMB_J - 8
CMB_K = N_EXPERTS * CMB_J
CMB_TILES = M // CMB_T


def _window_start(lo):
    return jnp.minimum((lo // 8) * 8, EROWS - CMB_J)


def _combine_kernel(starts_ref, nr_ref, pos_ref, y_hbm, ex_ref, x1_ref, g2_ref, lng_ref, lnb_ref, o_ref, wbuf, sem):
    i = pl.program_id(0)
    slot = i % 2

    def copies(tile, rnd, slot_):
        out = []
        for e in range(N_EXPERTS):
            st = _window_start(starts_ref[tile * N_EXPERTS + e] + rnd * CMB_STEP)
            out.append(pltpu.make_async_copy(y_hbm.at[e, pl.ds(pl.multiple_of(st, 8), CMB_J), :],
                                             wbuf.at[slot_, pl.ds(e * CMB_J, CMB_J), :], sem.at[slot_]))
        return out

    @pl.when(i == 0)
    def _():
        for cp in copies(0, 0, 0):
            cp.start()

    @pl.when(i + 1 < CMB_TILES)
    def _():
        for cp in copies(i + 1, 0, 1 - slot):
            cp.start()

    lane = lax.broadcasted_iota(jnp.int32, (1, LANES), 1)
    jmod = (lax.broadcasted_iota(jnp.int32, (1, CMB_K), 1) % CMB_J).astype(F32)
    pos = pos_ref[...]

    def scatter_round(rnd):
        lov = jnp.zeros((1, LANES), jnp.int32)
        for e in range(N_EXPERTS):
            lov = jnp.where(lane == e, starts_ref[i * N_EXPERTS + e] + rnd * CMB_STEP, lov)
        stv = _window_start(lov)
        valid = (pos >= lov) & (pos < lov + CMB_STEP) & (lane < N_EXPERTS)
        rel = jnp.where(valid, pos - stv, -1).astype(F32).astype(BF16)
        spread = jnp.dot(rel, ex_ref[...], preferred_element_type=F32)
        onehot = (spread == jmod).astype(BF16)
        lo, hi = _unpack_halves(wbuf[slot])
        return jnp.concatenate([jnp.dot(onehot, lo, preferred_element_type=F32),
                                jnp.dot(onehot, hi, preferred_element_type=F32)], axis=1)

    for cp in copies(i, 0, slot):
        cp.wait()
    moe = scatter_round(0)

    def extra(rnd, acc):
        for cp in copies(i, rnd, slot):
            cp.start()
        for cp in copies(i, rnd, slot):
            cp.wait()
        return acc + scatter_round(rnd)

    moe = lax.fori_loop(1, nr_ref[i], extra, moe)
    o_ref[...] = _ln(ALPHA * x1_ref[...] + g2_ref[...] * moe) * lng_ref[...] + lnb_ref[...]


def moe_combine(starts, rounds, pos, y, x1, mod_l, ln_g, ln_b):
    ex = np.zeros((LANES, CMB_K), np.float32)
    for e in range(N_EXPERTS):
        ex[e, e * CMB_J:(e + 1) * CMB_J] = 1.0
    lat_tiles = NL // CMB_T
    mod_row = lambda i: jnp.where(i < lat_tiles, i // (SEQ // CMB_T), B)
    row = lambda i, *_: (i, 0)
    const = lambda i, *_: (0, 0)
    grid_spec = pltpu.PrefetchScalarGridSpec(
        num_scalar_prefetch=2,
        grid=(CMB_TILES,),
        in_specs=[pl.BlockSpec((CMB_T, LANES), row), pl.BlockSpec(memory_space=pl.ANY),
                  pl.BlockSpec((LANES, CMB_K), const), pl.BlockSpec((CMB_T, D), row),
                  pl.BlockSpec((None, 1, D), lambda i, *_: (mod_row(i), 0, 5)),
                  pl.BlockSpec((1, D), const), pl.BlockSpec((1, D), const)],
        out_specs=pl.BlockSpec((CMB_T, D), row),
        scratch_shapes=[pltpu.VMEM((2, CMB_K, D // 2), jnp.int32), pltpu.SemaphoreType.DMA((2,))],
    )
    return pl.pallas_call(
        _combine_kernel,
        grid_spec=grid_spec,
        out_shape=jax.ShapeDtypeStruct((M, D), F32),
        compiler_params=_params("arbitrary"),
        name="moe_combine",
    )(starts, rounds, pos, y, jnp.asarray(ex, BF16), x1, mod_l, ln_g.reshape(1, D), ln_b.reshape(1, D))


def moe_route(aff):
    pos_l, cex_l, cin_l, idx_l, tw_l = moe_select(aff, SEQ, CAP_LAT, 0, 0)
    pos_c, cex_c, cin_c, idx_c, tw_c = moe_select(aff, CTX, CAP_CTX, NL, B * CAP_LAT)
    slot_major = lambda a, c: jnp.concatenate([a.transpose(1, 0, 2).reshape(N_EXPERTS, -1),
                                               c.transpose(1, 0, 2).reshape(N_EXPERTS, -1)], axis=1)
    idx = slot_major(idx_l, idx_c).reshape(N_EXPERTS, 1, EROWS)
    tw = slot_major(tw_l, tw_c).reshape(N_EXPERTS, EROWS, 1)
    pos = jnp.concatenate([pos_l, pos_c], axis=0)
    starts = jnp.concatenate([cex_l, cex_c], axis=0)[0::CMB_T, :N_EXPERTS]
    ends = jnp.concatenate([cin_l, cin_c], axis=0)[CMB_T - 1::CMB_T, :N_EXPERTS]
    rounds = jnp.maximum(jnp.max((ends - starts + CMB_STEP - 1) // CMB_STEP, axis=1), 1)
    return idx, tw, pos, starts.reshape(-1), rounds


HY_R = 256
HY_TC = 256


def _hy_na(L):
    return 2 * L // HY_R


def _hy_tables(L):
    na = _hy_na(L)
    nk = na // 2 + 1
    eye = np.eye(8)
    ka = np.arange(nk)

    def s1(n_a):
        ang = -2.0 * np.pi * np.outer(ka, np.arange(n_a)) / na
        return np.concatenate([np.kron(np.cos(ang), eye), np.kron(np.sin(ang), eye)], axis=0)

    r = np.arange(HY_R)
    ang = -2.0 * np.pi * np.outer(r, r) / HY_R
    fs = np.concatenate([np.cos(ang), np.sin(ang)], axis=0)
    tang = -2.0 * np.pi * ka[:, None] * r[None, :] / (2 * L)
    tw_r = np.repeat(np.cos(tang)[:, :, None], LANES, axis=2)
    tw_i = np.repeat(np.sin(tang)[:, :, None], LANES, axis=2)
    wgt = np.where((ka == 0) | (ka == na // 2), 1.0, 2.0)
    iang = 2.0 * np.pi * np.outer(np.arange(na // 2), ka) / na
    g = np.concatenate([np.kron(np.cos(iang) * wgt, eye), np.kron(-np.sin(iang) * wgt, eye)], axis=1)
    f = lambda a: jnp.asarray(a, F32)
    return dict(s1_half=f(s1(na // 2)), s1_full=f(s1(na)), fs=f(fs), tw_r=f(tw_r), tw_i=f(tw_i), g=f(g))


def _hy_stage1(x_ref, s1, ar_ref, ai_ref, n_a, nk, prec):
    tc = x_ref.shape[-1]

    def body(rh, carry):
        rows = pl.ds(pl.multiple_of(rh * 8, 8), 8)
        slab = x_ref[:, rows, :].reshape(8 * n_a, tc)
        if prec is None:
            slab = slab.astype(BF16)
        out = jnp.dot(s1, slab, precision=prec, preferred_element_type=F32)
        ar_ref[:, rows, :] = out[: 8 * nk].reshape(nk, 8, tc)
        ai_ref[:, rows, :] = out[8 * nk:].reshape(nk, 8, tc)
        return carry

    lax.fori_loop(0, HY_R // 8, body, 0)


def _hy_cmul_tw(xr, xi, twr, twi, conj):
    reps = xr.shape[-1] // LANES
    if reps > 1:
        twr = jnp.concatenate([twr] * reps, axis=1)
        twi = jnp.concatenate([twi] * reps, axis=1)
    if conj:
        return xr * twr + xi * twi, xi * twr - xr * twi
    return xr * twr - xi * twi, xi * twr + xr * twi


def _hy_dft256(fs, xr, xi, inverse, prec):
    if prec is None:
        xr, xi = xr.astype(BF16), xi.astype(BF16)
    t1 = jnp.dot(fs, xr, precision=prec, preferred_element_type=F32)
    t2 = jnp.dot(fs, xi, precision=prec, preferred_element_type=F32)
    if inverse:
        return t1[:HY_R] + t2[HY_R:], t2[:HY_R] - t1[HY_R:]
    return t1[:HY_R] - t2[HY_R:], t2[:HY_R] + t1[HY_R:]


def _hy_filter_kernel(w1t_ref, w1c_ref, w1s_ref, b1_ref, w2_ref, b2_ref, w3_ref, freq_ref, dl_ref, o_ref, norm_ref, *, L):
    i = pl.program_id(0)
    tn = o_ref.shape[0]
    n = i * tn + lax.broadcasted_iota(jnp.int32, (tn, 1), 0)
    t = jnp.where(n < L, n, 2 * L - n).astype(F32)
    t_unit = t / max(L - 1, 1)
    j = lax.broadcasted_iota(jnp.int32, (1, HY_BANDS), 1).astype(F32)
    bands = 1e-4 + j * ((HY_BANDS - 1 - 1e-4) / (HY_BANDS - 1))
    ang = (2.0 * math.pi / L) * t * bands
    f = freq_ref[...]
    dot = functools.partial(jnp.dot, precision=HIGHEST, preferred_element_type=F32)
    h = t_unit * w1t_ref[...] + dot(jnp.cos(ang), w1c_ref[...]) - dot(jnp.sin(ang), w1s_ref[...])
    h = jnp.sin(f * (h + b1_ref[...]))
    h = jnp.sin(f * (dot(h, w2_ref[...]) + b2_ref[...]))
    hh = dot(h, w3_ref[...])
    half = 2 * HY_W
    k = jnp.where(n < L, hh[:, :half], hh[:, half:]) * jnp.exp(-t_unit * dl_ref[...])
    k = jnp.where(n == L, 0.0, k)
    o_ref[...] = k

    @pl.when(i == 0)
    def _():
        norm_ref[...] = jnp.zeros_like(norm_ref)

    norm_ref[...] += jnp.sum(jnp.abs(k), axis=0, keepdims=True)


def hyena_filter_time(L, w1, b1, w2, b2, w3, freq):
    tn = min(512, 2 * L)
    deltas = np.abs(np.linspace(HY_MIN_DECAY, HY_MAX_DECAY, HY_W, dtype=np.float32))
    dl = jnp.asarray(np.tile(deltas, 2)[None, :], F32)
    const = lambda shape: pl.BlockSpec(shape, lambda i: (0, 0))
    return pl.pallas_call(
        functools.partial(_hy_filter_kernel, L=L),
        grid=(2 * L // tn,),
        in_specs=[const((1, HY_HID)), const((HY_BANDS, HY_HID)), const((HY_BANDS, HY_HID)), const((1, HY_HID)),
                  const((HY_HID, HY_HID)), const((1, HY_HID)), const((HY_HID, 4 * HY_W)), const((1, HY_HID)),
                  const((1, 2 * HY_W))],
        out_specs=(pl.BlockSpec((tn, 2 * HY_W), lambda i: (i, 0)), const((1, 2 * HY_W))),
        out_shape=(jax.ShapeDtypeStruct((2 * L, 2 * HY_W), F32), jax.ShapeDtypeStruct((1, 2 * HY_W), F32)),
        compiler_params=_params("arbitrary"),
        name="hyena_filter_time",
    )(w1[0:1], w1[1:1 + HY_BANDS], w1[1 + HY_BANDS:], b1.reshape(1, -1), w2, b2.reshape(1, -1), w3,
      freq.reshape(1, -1), dl)


def _hy_spectrum_kernel(k_ref, norm_ref, s1_ref, fs_ref, twr_ref, twi_ref, kr_ref, ki_ref, ar_s, ai_s, *, L):
    ka = pl.program_id(1)
    na = _hy_na(L)
    nk = na // 2 + 1

    @pl.when(ka == 0)
    def _():
        _hy_stage1(k_ref, s1_ref[...], ar_s, ai_s, na, nk, HIGHEST)

    xr, xi = _hy_cmul_tw(ar_s[ka], ai_s[ka], twr_ref[...], twi_ref[...], False)
    br, bi = _hy_dft256(fs_ref[...], xr, xi, False, HIGHEST)
    scale = 1.0 / (2.0 * L * norm_ref[...])
    kr_ref[...] = br * scale
    ki_ref[...] = bi * scale


def hyena_filter_spectrum(L, k_time, norm, tabs):
    na = _hy_na(L)
    nk = na // 2 + 1
    tc = HY_TC
    out = jax.ShapeDtypeStruct((nk, HY_R, 2 * HY_W), F32)
    return pl.pallas_call(
        functools.partial(_hy_spectrum_kernel, L=L),
        grid=(2 * HY_W // tc, nk),
        in_specs=[pl.BlockSpec((na, HY_R, tc), lambda j, ka: (0, 0, j)),
                  pl.BlockSpec((1, tc), lambda j, ka: (0, j)),
                  pl.BlockSpec((16 * nk, 8 * na), lambda j, ka: (0, 0)),
                  pl.BlockSpec((2 * HY_R, HY_R), lambda j, ka: (0, 0)),
                  pl.BlockSpec((None, HY_R, LANES), lambda j, ka: (ka, 0, 0)),
                  pl.BlockSpec((None, HY_R, LANES), lambda j, ka: (ka, 0, 0))],
        out_specs=(pl.BlockSpec((None, HY_R, tc), lambda j, ka: (ka, 0, j)),
                   pl.BlockSpec((None, HY_R, tc), lambda j, ka: (ka, 0, j))),
        out_shape=(out, out),
        scratch_shapes=[pltpu.VMEM((nk, HY_R, tc), F32), pltpu.VMEM((nk, HY_R, tc), F32)],
        compiler_params=_params("arbitrary", "arbitrary"),
        name="hyena_filter_spectrum",
    )(k_time.reshape(na, HY_R, 2 * HY_W), norm, tabs["s1_full"], tabs["fs"], tabs["tw_r"], tabs["tw_i"])


def _hy_conv_kernel(v_ref, x1_ref, x2_ref, cw_ref, cb_ref, skip_ref, s1_ref, fs_ref, g_ref, twr_ref, twi_ref,
                    kr_ref, ki_ref, o_ref, xin_s, ar_s, ai_s, *, L):
    s = pl.program_id(2)
    na = _hy_na(L)
    nh = na // 2
    nk = nh + 1
    tc = HY_TC
    ka = s % nk

    def short_conv(ref, part):
        row = lax.broadcasted_iota(jnp.int32, (L, 1), 0)
        u = ref[...].astype(F32)
        up = jnp.where(row == 0, 0.0, pltpu.roll(u, 1, 0))
        dn = jnp.where(row == L - 1, 0.0, pltpu.roll(u, L - 1, 0))
        w = cw_ref[part]
        y = w[0:1, :] * up + w[1:2, :] * u + w[2:3, :] * dn + cb_ref[part]
        return y.reshape(nh, HY_R, tc)

    @pl.when(s == 0)
    def _():
        xin_s[...] = short_conv(v_ref, 0)

    @pl.when(ka == 0)
    def _():
        _hy_stage1(xin_s, s1_ref[...].astype(BF16), ar_s, ai_s, nh, nk, None)

    fs = fs_ref[...].astype(BF16)
    twr, twi = twr_ref[...], twi_ref[...]
    xr, xi = _hy_cmul_tw(ar_s[ka], ai_s[ka], twr, twi, False)
    br, bi = _hy_dft256(fs, xr, xi, False, None)
    kr, ki = kr_ref[...], ki_ref[...]
    pr, pi_ = br * kr - bi * ki, br * ki + bi * kr
    qr, qi = _hy_dft256(fs, pr, pi_, True, None)
    qr, qi = _hy_cmul_tw(qr, qi, twr, twi, True)
    ar_s[ka] = qr
    ai_s[ka] = qi

    def finish(order):
        gm = g_ref[...].astype(BF16)
        sk = skip_ref[order]

        def body(rh, carry):
            rows = pl.ds(pl.multiple_of(rh * 8, 8), 8)
            q = jnp.concatenate([ar_s[:, rows, :].reshape(8 * nk, tc), ai_s[:, rows, :].reshape(8 * nk, tc)], axis=0)
            y = jnp.dot(gm, q.astype(BF16), preferred_element_type=F32).reshape(nh, 8, tc)
            xin_s[:, rows, :] = y + xin_s[:, rows, :] * sk
            return carry

        lax.fori_loop(0, HY_R // 8, body, 0)

    @pl.when(s == nk - 1)
    def _():
        finish(0)
        xin_s[...] = short_conv(x1_ref, 1) * xin_s[...]

    @pl.when(s == 2 * nk - 1)
    def _():
        finish(1)
        o_ref[...] = (short_conv(x2_ref, 2) * xin_s[...]).reshape(L, tc).astype(o_ref.dtype)


def hyena_conv(p_hy, row0, L, conv_w, conv_b, skip, kf_r, kf_i, tabs):
    na = _hy_na(L)
    nh, nk = na // 2, na // 2 + 1
    tc = HY_TC
    nct = HY_W // tc
    rb0 = row0 // L
    cw = conv_w.reshape(3, 3, HY_W).transpose(1, 0, 2)
    cb = conv_b.reshape(3, 1, HY_W)
    col = lambda part: pl.BlockSpec((L, tc), lambda b, c, s, part=part: (rb0 + b, part * nct + c))
    const2 = lambda shape: pl.BlockSpec(shape, lambda b, c, s: (0, 0))
    tw_spec = pl.BlockSpec((None, HY_R, LANES), lambda b, c, s: (s % nk, 0, 0))
    kf_spec = pl.BlockSpec((None, HY_R, tc), lambda b, c, s: (s % nk, 0, (s // nk) * nct + c))
    return pl.pallas_call(
        functools.partial(_hy_conv_kernel, L=L),
        grid=(B, nct, 2 * nk),
        in_specs=[col(0), col(1), col(2),
                  pl.BlockSpec((3, 3, tc), lambda b, c, s: (0, 0, c)),
                  pl.BlockSpec((3, 1, tc), lambda b, c, s: (0, 0, c)),
                  pl.BlockSpec((2, 1, tc), lambda b, c, s: (0, 0, c)),
                  const2((16 * nk, 8 * nh)), const2((2 * HY_R, HY_R)), const2((8 * nh, 16 * nk)),
                  tw_spec, tw_spec, kf_spec, kf_spec],
        out_specs=pl.BlockSpec((L, tc), lambda b, c, s: (b, c)),
        out_shape=jax.ShapeDtypeStruct((B * L, HY_W), BF16),
        scratch_shapes=[pltpu.VMEM((nh, HY_R, tc), F32), pltpu.VMEM((nk, HY_R, tc), F32),
                        pltpu.VMEM((nk, HY_R, tc), F32)],
        compiler_params=_params("arbitrary", "arbitrary", "arbitrary"),
        name="hyena_conv_%d" % L,
    )(p_hy, p_hy, p_hy, cw, cb, skip.reshape(2, 1, HY_W), tabs["s1_half"], tabs["fs"], tabs["g"],
      tabs["tw_r"], tabs["tw_i"], kf_r, kf_i)


def hyena_branch(p_hy, row0, L, conv_w, conv_b, w1, b1, w2, b2, w3, freq, skip):
    tabs = _hy_tables(L)
    k_time, norm = hyena_filter_time(L, w1, b1, w2, b2, w3, freq)
    kf_r, kf_i = hyena_filter_spectrum(L, k_time, norm, tabs)
    return hyena_conv(p_hy, row0, L, conv_w, conv_b, skip, kf_r, kf_i, tabs)


def kernel(x, c, ctx, c_ctx, w_mod, b_mod, w_in, hy_conv_w, hy_conv_b, hy_w1, hy_b1, hy_w2, hy_b2, hy_w3, hy_freq, hy_skip, ssd_conv_w, ssd_conv_b, ssd_a_log, ssd_dt_bias, ssd_d, ssd_norm_g, da_lambda, da_subln_g, w_out, ln1_g, ln1_b, router, w_gate, w_up, w_down, ln2_g, ln2_b):
    xa = jnp.concatenate([x.reshape(NL, D), ctx.reshape(NC, D)], axis=0)
    cc = jnp.concatenate([c, c_ctx[None, :], jnp.zeros((8 - B - 1, D), F32)], axis=0)
    mod = modulation(cc, w_mod, b_mod).reshape(DEPTH, 8, 1, 6 * D)
    rope = rope_tables()
    o_ssd = HY_PROJ
    o_xbc = o_ssd + SSD_W
    o_dt = o_xbc + SSD_XBC
    o_da = o_dt + 2 * SSD_H
    w_hy = w_in[:, :, :o_ssd].astype(BF16)
    w_ssd = jnp.concatenate([w_in[:, :, o_xbc:o_dt], w_in[:, :, o_ssd:o_xbc], w_in[:, :, o_dt:o_da],
                             jnp.zeros((DEPTH, D, LANES - 2 * SSD_H), F32)], axis=2).astype(BF16)
    w_da = w_in[:, :, o_da:].astype(BF16)
    w_out_b = w_out.astype(BF16)
    r_pad = jnp.pad(router, ((0, 0), (0, 0), (0, LANES - N_EXPERTS))).astype(BF16)
    for l in range(DEPTH):
        lam_init = 0.8 - 0.6 * math.exp(-0.3 * l)
        p_hy = in_projection(xa, mod[l], w_hy, l, "hy")
        zx, dt = in_projection(xa, mod[l], w_ssd, l, "ssd")
        da = in_projection(xa, mod[l], w_da, l, "da", rope)

        hy_args = (hy_conv_w[l], hy_conv_b[l], hy_w1[l], hy_b1[l], hy_w2[l], hy_b2[l], hy_w3[l], hy_freq[l], hy_skip[l])
        with_ctx = l < DEPTH - 1
        y_hy_ctx = hyena_branch(p_hy, NL, CTX, *hy_args) if with_ctx else jnp.zeros((NC, HY_W), BF16)
        y_hy = jnp.concatenate([hyena_branch(p_hy, 0, SEQ, *hy_args), y_hy_ctx], axis=0)

        yf, yb = ssd_scan(zx, dt, ssd_conv_w[l], ssd_conv_b[l], ssd_a_log[l], ssd_dt_bias[l], ssd_d[l])
        y_ssd = ssd_gate(yf, yb, zx, ssd_norm_g[l])

        lp = da_lambda[l].astype(F32)
        lam = jnp.exp(jnp.sum(lp[0] * lp[1])) - jnp.exp(jnp.sum(lp[2] * lp[3])) + lam_init
        y_da_ctx = diff_attention(da, lam, da_subln_g[l], lam_init, True) if with_ctx else jnp.zeros((NC, DA_W), BF16)
        y_da = jnp.concatenate([diff_attention(da, lam, da_subln_g[l], lam_init, False), y_da_ctx], axis=0)

        x1, h2, aff = out_projection(y_hy, y_ssd, y_da, w_out_b, l, xa, mod[l], ln1_g[l], ln1_b[l], r_pad)
        idx, tw, pos, starts, rounds = moe_route(aff)
        y = expert_ffn(idx, h2, tw, w_gate, w_up, w_down, l)
        xa = moe_combine(starts, rounds, pos, y, x1, mod[l], ln2_g[l], ln2_b[l])
    return xa[:NL].reshape(B, SEQ, D)
```
